```python
import math
import jax, jax.numpy as jnp
from jax import lax
import numpy as np

D_MODEL = 1024
BATCH = 4
SEQ = 8192
DEPTH = 2

BRANCH_WIDTH = D_MODEL
N_BRANCHES = 3
CHUNK = 128
GMLP_GROUPS = 8
GMLP_GROUP_DIM = BRANCH_WIDTH // GMLP_GROUPS
CONV_WIDTH = 31
XATTN_HEADS = 4
XATTN_HEAD_DIM = BRANCH_WIDTH // XATTN_HEADS
MEM_LEN = 256
OFF_A_U = 0
OFF_A_V = OFF_A_U + BRANCH_WIDTH
OFF_A_G = OFF_A_V + BRANCH_WIDTH
OFF_B_A = OFF_A_G + BRANCH_WIDTH
OFF_B_B = OFF_B_A + BRANCH_WIDTH
OFF_B_G = OFF_B_B + BRANCH_WIDTH
OFF_C_Q = OFF_B_G + BRANCH_WIDTH
OFF_C_G = OFF_C_Q + BRANCH_WIDTH
OFF_MERGE = OFF_C_G + BRANCH_WIDTH
N_IN = OFF_MERGE + N_BRANCHES * D_MODEL
RMS_EPS = 1e-6
LN_EPS = 1e-5

kernel_name = "hybrid_gmlp_conformer_xattn_gated_merge"


def rms_norm(x, g):
    xf = x.astype(jnp.float32)
    y = xf * lax.rsqrt(jnp.mean(xf * xf, axis=-1, keepdims=True) + RMS_EPS)
    return (y * g.astype(jnp.float32)).astype(x.dtype)


def layer_norm(x, g, b):
    xf = x.astype(jnp.float32)
    mu = jnp.mean(xf, axis=-1, keepdims=True)
    xc = xf - mu
    var = jnp.mean(xc * xc, axis=-1, keepdims=True)
    y = xc * lax.rsqrt(var + LN_EPS)
    return (y * g.astype(jnp.float32) + b.astype(jnp.float32)).astype(x.dtype)


def gmlp_spatial_gate(u, v, w_s, b_s):
    bsz, seq, width = v.shape
    n_chunks = seq // CHUNK
    mask = jnp.tril(jnp.ones((CHUNK, CHUNK), dtype=bool))
    ws = jnp.where(mask[None], w_s, 0.0).astype(v.dtype)
    vr = v.reshape(bsz, n_chunks, CHUNK, GMLP_GROUPS, GMLP_GROUP_DIM)
    sv = jnp.einsum('gts,bcsgd->bctgd', ws, vr) + b_s.T.astype(v.dtype)[None, None, :, :, None]
    return u * sv.reshape(bsz, seq, width)


def causal_depthwise_conv(x, w, b):
    k = w.astype(x.dtype)[:, None, :]
    y = lax.conv_general_dilated(
        x, k, window_strides=(1,), padding=[(CONV_WIDTH - 1, 0)],
        dimension_numbers=('NWC', 'WIO', 'NWC'), feature_group_count=x.shape[-1])
    return y + b.astype(x.dtype)


def memory_cross_attention(q, mem_n, w_kv):
    bsz, seq, _ = q.shape
    kv = jnp.einsum('bmd,de->bme', mem_n, w_kv).reshape(bsz, MEM_LEN, 2, XATTN_HEADS, XATTN_HEAD_DIM)
    k, v = kv[:, :, 0], kv[:, :, 1]
    qh = q.reshape(bsz, seq, XATTN_HEADS, XATTN_HEAD_DIM)
    scores = jnp.einsum('bshd,bmhd->bhsm', qh.astype(jnp.float32), k.astype(jnp.float32))
    probs = jax.nn.softmax(scores * (1.0 / math.sqrt(XATTN_HEAD_DIM)), axis=-1).astype(v.dtype)
    out = jnp.einsum('bhsm,bmhd->bshd', probs, v)
    return out.reshape(bsz, seq, BRANCH_WIDTH)


def setup_inputs(seed: int = 0) -> dict:
    key = jax.random.key(seed)
    ks = jax.random.split(key, 20)
    f32 = jnp.float32
    nrm = lambda k, shape, scale: jax.random.normal(k, shape, f32) * scale
    return {
        "x": nrm(ks[0], (BATCH, SEQ, D_MODEL), 1.0),
        "mem": nrm(ks[1], (BATCH, MEM_LEN, D_MODEL), 1.0),
        "norm_g": 1.0 + nrm(ks[2], (DEPTH, D_MODEL), 0.02),
        "mem_norm_g": 1.0 + nrm(ks[3], (DEPTH, D_MODEL), 0.02),
        "w_in": nrm(ks[4], (DEPTH, D_MODEL, N_IN), D_MODEL ** -0.5),
        "gmlp_ln_g": 1.0 + nrm(ks[5], (DEPTH, BRANCH_WIDTH), 0.02),
        "gmlp_ln_b": nrm(ks[6], (DEPTH, BRANCH_WIDTH), 0.02),
        "w_s": nrm(ks[7], (DEPTH, GMLP_GROUPS, CHUNK, CHUNK), CHUNK ** -0.5),
        "b_s": 1.0 + nrm(ks[8], (DEPTH, GMLP_GROUPS, CHUNK), 0.02),
        "conv_w": nrm(ks[9], (DEPTH, CONV_WIDTH, BRANCH_WIDTH), CONV_WIDTH ** -0.5),
        "conv_b": nrm(ks[10], (DEPTH, BRANCH_WIDTH), 0.02),
        "conv_ln_g": 1.0 + nrm(ks[11], (DEPTH, BRANCH_WIDTH), 0.02),
        "conv_ln_b": nrm(ks[12], (DEPTH, BRANCH_WIDTH), 0.02),
        "w_kv": nrm(ks[13], (DEPTH, D_MODEL, 2 * BRANCH_WIDTH), D_MODEL ** -0.5),
        "w_branch": nrm(ks[14], (DEPTH, N_BRANCHES, BRANCH_WIDTH, D_MODEL), BRANCH_WIDTH ** -0.5),
        "w_out": nrm(ks[15], (DEPTH, D_MODEL, D_MODEL), D_MODEL ** -0.5),
        "final_norm_g": 1.0 + nrm(ks[16], (D_MODEL,), 0.02),
    }


def reference(x, mem, norm_g, mem_norm_g, w_in, gmlp_ln_g, gmlp_ln_b, w_s, b_s,
              conv_w, conv_b, conv_ln_g, conv_ln_b, w_kv, w_branch, w_out, final_norm_g):
    W = BRANCH_WIDTH
    for l in range(DEPTH):
        h = rms_norm(x, norm_g[l])
        z = jnp.einsum('bsd,de->bse', h, w_in[l])

        u = jax.nn.gelu(z[..., OFF_A_U:OFF_A_U + W])
        v = layer_norm(jax.nn.gelu(z[..., OFF_A_V:OFF_A_V + W]), gmlp_ln_g[l], gmlp_ln_b[l])
        br_a = gmlp_spatial_gate(u, v, w_s[l], b_s[l]) * jax.nn.silu(z[..., OFF_A_G:OFF_A_G + W])

        glu = z[..., OFF_B_A:OFF_B_A + W] * jax.nn.sigmoid(z[..., OFF_B_B:OFF_B_B + W])
        c = causal_depthwise_conv(glu, conv_w[l], conv_b[l])
        c = jax.nn.silu(layer_norm(c, conv_ln_g[l], conv_ln_b[l]))
        br_b = c * jax.nn.silu(z[..., OFF_B_G:OFF_B_G + W])

        mem_n = rms_norm(mem, mem_norm_g[l])
        att = memory_cross_attention(z[..., OFF_C_Q:OFF_C_Q + W], mem_n, w_kv[l])
        br_c = att * jax.nn.silu(z[..., OFF_C_G:OFF_C_G + W])

        branches = jnp.stack([br_a, br_b, br_c], axis=2)
        proj = jnp.einsum('bsnw,nwd->bsnd', branches, w_branch[l])
        gates = jax.nn.sigmoid(z[..., OFF_MERGE:OFF_MERGE + N_BRANCHES * D_MODEL]).reshape(
            x.shape[0], x.shape[1], N_BRANCHES, D_MODEL)
        merged = jnp.einsum('bsnd,bsnd->bsd', gates, proj)
        x = x + jnp.einsum('bsd,de->bse', merged, w_out[l])
    return rms_norm(x, final_norm_g)
```

```python
import functools

import jax
import jax.numpy as jnp
from jax import lax
from jax.experimental import pallas as pl
from jax.experimental.pallas import tpu as pltpu

D_MODEL = 1024
WIDTH = 1024
CHUNK = 128
GROUPS = 8
GROUP_DIM = WIDTH // GROUPS
CONV_WIDTH = 31
HEADS = 4
HEAD_DIM = WIDTH // HEADS
MEM_LEN = 256
N_SLABS = 11
N_IN = N_SLABS * WIDTH
RMS_EPS = 1e-6
LN_EPS = 1e-5

SLAB_A_U, SLAB_A_V, SLAB_A_G, SLAB_B_A, SLAB_B_B, SLAB_B_G, SLAB_C_Q, SLAB_C_G = range(8)
SLAB_MERGE = 8

LANES = 128
SUBLANES = 8
BF16_ROWS = 16
HALO = 32
CONV_ROWS = 32
TILE_TOKENS = 256
VMEM_LIMIT_BYTES = 56 * 1024 * 1024

_BF16 = jnp.bfloat16
_F32 = jnp.float32


def _gelu_tanh(x):
    return 0.5 * x * (1.0 + jnp.tanh(0.7978845608028654 * (x + 0.044715 * (x * x * x))))


def _sigmoid(x):
    return 0.5 + 0.5 * jnp.tanh(0.5 * x)


def _silu(x):
    hx = 0.5 * x
    return hx + hx * jnp.tanh(hx)


def _layer_norm_rows(x, g, b):
    mu = jnp.mean(x, axis=-1, keepdims=True)
    xc = x - mu
    var = jnp.mean(xc * xc, axis=-1, keepdims=True)
    return xc * lax.rsqrt(var + LN_EPS) * g + b


def _rms_norm_rows(x, g):
    return x * lax.rsqrt(jnp.mean(x * x, axis=-1, keepdims=True) + RMS_EPS) * g


def _dot(a, b):
    return jnp.dot(a, b, preferred_element_type=_F32)


def _kv_kernel(mem_ref, g_ref, wkv_ref, k_ref, v_ref):
    mem_n = _rms_norm_rows(mem_ref[...], g_ref[...])
    kv = _dot(mem_n.astype(_BF16), wkv_ref[...])
    k_ref[...] = (kv[:, :WIDTH] * (HEAD_DIM ** -0.5)).astype(_BF16)
    v_ref[...] = kv[:, WIDTH:].astype(_BF16)


def _keys_values(mem, mem_norm_g, w_kv_bf16):
    depth = w_kv_bf16.shape[0]
    batch = mem.shape[0]
    out = jax.ShapeDtypeStruct((depth, batch, MEM_LEN, WIDTH), _BF16)
    kv_spec = pl.BlockSpec((None, None, MEM_LEN, WIDTH), lambda l, b: (l, b, 0, 0))
    return pl.pallas_call(
        _kv_kernel,
        grid=(depth, batch),
        in_specs=[
            pl.BlockSpec((None, MEM_LEN, D_MODEL), lambda l, b: (b, 0, 0)),
            pl.BlockSpec((None, 1, D_MODEL), lambda l, b: (l, 0, 0)),
            pl.BlockSpec((None, D_MODEL, 2 * WIDTH), lambda l, b: (l, 0, 0)),
        ],
        out_specs=[kv_spec, kv_spec],
        out_shape=[out, out],
        compiler_params=pltpu.CompilerParams(dimension_semantics=("arbitrary", "arbitrary")),
        name="keys_values",
    )(mem, mem_norm_g.reshape(depth, 1, D_MODEL), w_kv_bf16)


def _layer_kernel(x_ref, k_ref, v_ref, ng_ref, win_ref, lag_ref, lab_ref, ws_ref, bs_ref,
                  cw_ref, cb_ref, lbg_ref, lbb_ref, wb_ref, wo_ref, fg_ref,
                  o_ref,
                  h_s, z0, z1, z2, zm, t0, t1, br, pj, acc, cbuf, *, tm, final):
    row_blocks = range(0, tm, SUBLANES)

    def slab(j):
        return _dot(h_s[...], win_ref[:, j * WIDTH:(j + 1) * WIDTH])

    def rows(ref, r, n=SUBLANES):
        return ref[r:r + n, :]

    @pl.when(pl.program_id(1) == 0)
    def _():
        cbuf[0:HALO, :] = jnp.zeros((HALO, WIDTH), _F32)

    ng = ng_ref[...]
    for r in range(0, tm, BF16_ROWS):
        h_s[r:r + BF16_ROWS, :] = _rms_norm_rows(x_ref[r:r + BF16_ROWS, :], ng).astype(_BF16)

    z0[...] = slab(SLAB_A_U)
    z1[...] = slab(SLAB_A_V)
    z2[...] = slab(SLAB_A_G)
    zm[...] = slab(SLAB_MERGE + 0)
    lag, lab = lag_ref[...], lab_ref[...]
    for r in row_blocks:
        t0[r:r + SUBLANES, :] = _layer_norm_rows(_gelu_tanh(rows(z1, r)), lag, lab)
    tri = (lax.broadcasted_iota(jnp.int32, (CHUNK, CHUNK), 0)
           >= lax.broadcasted_iota(jnp.int32, (CHUNK, CHUNK), 1))
    for g in range(GROUPS):
        w_g = jnp.where(tri, ws_ref[g], jnp.zeros((), _BF16))
        cols = slice(g * GROUP_DIM, (g + 1) * GROUP_DIM)
        for c in range(0, tm, CHUNK):
            t1[c:c + CHUNK, cols] = _dot(w_g, t0[c:c + CHUNK, cols].astype(_BF16))
    for r in row_blocks:
        rc = r % CHUNK
        sv = rows(t1, r) + bs_ref[rc:rc + SUBLANES, :]
        br[r:r + SUBLANES, :] = _gelu_tanh(rows(z0, r)) * sv * _silu(rows(z2, r))
    pj[...] = _dot(br[...].astype(_BF16), wb_ref[0])
    for r in row_blocks:
        acc[r:r + SUBLANES, :] = _sigmoid(rows(zm, r)) * rows(pj, r)

    z0[...] = slab(SLAB_B_A)
    z1[...] = slab(SLAB_B_B)
    z2[...] = slab(SLAB_B_G)
    zm[...] = slab(SLAB_MERGE + 1)
    for r in row_blocks:
        cbuf[HALO + r:HALO + r + SUBLANES, :] = rows(z0, r) * _sigmoid(rows(z1, r))
    first = HALO - (CONV_WIDTH - 1)
    for j in range(0, WIDTH, LANES):
        w_taps = [cw_ref[k:k + 1, j:j + LANES] for k in range(CONV_WIDTH)]
        for r in range(0, tm, CONV_ROWS):
            y = cbuf[r + first:r + first + CONV_ROWS, j:j + LANES] * w_taps[0]
            for k in range(1, CONV_WIDTH):
                y = y + cbuf[r + first + k:r + first + k + CONV_ROWS, j:j + LANES] * w_taps[k]
            t0[r:r + CONV_ROWS, j:j + LANES] = y
    cbuf[0:HALO, :] = cbuf[tm:tm + HALO, :]
    cb, lbg, lbb = cb_ref[...], lbg_ref[...], lbb_ref[...]
    for r in row_blocks:
        c = _silu(_layer_norm_rows(rows(t0, r) + cb, lbg, lbb))
        br[r:r + SUBLANES, :] = c * _silu(rows(z2, r))
    pj[...] = _dot(br[...].astype(_BF16), wb_ref[1])
    for r in row_blocks:
        acc[r:r + SUBLANES, :] = rows(acc, r) + _sigmoid(rows(zm, r)) * rows(pj, r)

    z0[...] = slab(SLAB_C_Q)
    z2[...] = slab(SLAB_C_G)
    zm[...] = slab(SLAB_MERGE + 2)
    nt_dims = (((1,), (1,)), ((), ()))
    for hd in range(HEADS):
        cols = slice(hd * HEAD_DIM, (hd + 1) * HEAD_DIM)
        t1[:, cols] = lax.dot_general(z0[:, cols].astype(_BF16), k_ref[:, cols], nt_dims,
                                      preferred_element_type=_F32)
    for r in row_blocks:
        for hd in range(HEADS):
            cols = slice(hd * MEM_LEN, (hd + 1) * MEM_LEN)
            s = t1[r:r + SUBLANES, cols]
            p = jnp.exp(s - jnp.max(s, axis=-1, keepdims=True))
            t0[r:r + SUBLANES, cols] = p * (1.0 / jnp.sum(p, axis=-1, keepdims=True))
    for hd in range(HEADS):
        pcols = slice(hd * MEM_LEN, (hd + 1) * MEM_LEN)
        cols = slice(hd * HEAD_DIM, (hd + 1) * HEAD_DIM)
        pj[:, cols] = _dot(t0[:, pcols].astype(_BF16), v_ref[:, cols])
    for r in row_blocks:
        br[r:r + SUBLANES, :] = rows(pj, r) * _silu(rows(z2, r))
    t1[...] = _dot(br[...].astype(_BF16), wb_ref[2])
    for r in row_blocks:
        acc[r:r + SUBLANES, :] = rows(acc, r) + _sigmoid(rows(zm, r)) * rows(t1, r)

    pj[...] = _dot(acc[...].astype(_BF16), wo_ref[...])
    fg = fg_ref[...]
    for r in row_blocks:
        y = x_ref[r:r + SUBLANES, :] + rows(pj, r)
        if final:
            y = _rms_norm_rows(y, fg)
        o_ref[r:r + SUBLANES, :] = y


def _layer(x, k_all, v_all, layer, final, norm_g, w_in, lag, lab, w_s, bs_full, conv_w, conv_b,
           lbg, lbb, w_branch, w_out, final_g):
    batch, seq, _ = x.shape
    tm = TILE_TOKENS
    assert seq % tm == 0 and tm % CHUNK == 0 and tm % CONV_ROWS == 0

    def resident(shape):
        nd = len(shape)
        return pl.BlockSpec((None,) + shape, lambda b, s: (layer,) + (0,) * nd,
                            pipeline_mode=pl.Buffered(1))

    tile = pl.BlockSpec((None, tm, D_MODEL), lambda b, s: (b, s, 0))
    kv_spec = pl.BlockSpec((None, None, MEM_LEN, WIDTH), lambda b, s: (layer, b, 0, 0))
    row = (1, WIDTH)
    big = pltpu.VMEM((tm, WIDTH), _F32)
    return pl.pallas_call(
        functools.partial(_layer_kernel, tm=tm, final=final),
        grid=(batch, seq // tm),
        in_specs=[
            tile, kv_spec, kv_spec,
            resident(row),
            resident((D_MODEL, N_IN)),
            resident(row), resident(row),
            resident((GROUPS, CHUNK, CHUNK)),
            resident((CHUNK, WIDTH)),
            resident((CONV_WIDTH, WIDTH)),
            resident(row), resident(row), resident(row),
            resident((3, WIDTH, D_MODEL)),
            resident((D_MODEL, D_MODEL)),
            pl.BlockSpec((1, D_MODEL), lambda b, s: (0, 0), pipeline_mode=pl.Buffered(1)),
        ],
        out_specs=tile,
        out_shape=jax.ShapeDtypeStruct(x.shape, x.dtype),
        scratch_shapes=[pltpu.VMEM((tm, D_MODEL), _BF16)] + [big] * 9
                       + [pltpu.VMEM((tm + HALO, WIDTH), _F32)],
        compiler_params=pltpu.CompilerParams(
            dimension_semantics=("arbitrary", "arbitrary"),
            vmem_limit_bytes=VMEM_LIMIT_BYTES),
        name=f"trunk_layer_{layer}",
    )(x, k_all, v_all, norm_g, w_in, lag, lab, w_s, bs_full, conv_w, conv_b, lbg, lbb,
      w_branch, w_out, final_g)


def kernel(x, mem, norm_g, mem_norm_g, w_in, gmlp_ln_g, gmlp_ln_b, w_s, b_s, conv_w, conv_b,
           conv_ln_g, conv_ln_b, w_kv, w_branch, w_out, final_norm_g):
    depth = w_in.shape[0]
    vec = lambda a: a.reshape(depth, 1, a.shape[-1])
    bs_full = jnp.repeat(jnp.swapaxes(b_s, 1, 2), GROUP_DIM, axis=2)
    k_all, v_all = _keys_values(mem, mem_norm_g, w_kv.astype(_BF16))
    w_in_b, w_s_b = w_in.astype(_BF16), w_s.astype(_BF16)
    w_branch_b, w_out_b = w_branch.astype(_BF16), w_out.astype(_BF16)
    final_g = final_norm_g.reshape(1, D_MODEL)
    for layer in range(depth):
        x = _layer(x, k_all, v_all, layer, layer == depth - 1, vec(norm_g), w_in_b,
                   vec(gmlp_ln_g), vec(gmlp_ln_b), w_s_b, bs_full, conv_w, vec(conv_b),
                   vec(conv_ln_g), vec(conv_ln_b), w_branch_b, w_out_b, final_g)
    return x
```

```python
import functools

import jax
import jax.numpy as jnp
from jax import lax
from jax.experimental import pallas as pl
from jax.experimental.pallas import tpu as pltpu

D_MODEL = 1024
WIDTH = 1024
CHUNK = 128
GROUPS = 8
GROUP_DIM = WIDTH // GROUPS
CONV_WIDTH = 31
HEADS = 4
HEAD_DIM = WIDTH // HEADS
MEM_LEN = 256
N_SLABS = 11
N_IN = N_SLABS * WIDTH
RMS_EPS = 1e-6
LN_EPS = 1e-5

SLAB_A_U, SLAB_A_V, SLAB_A_G, SLAB_B_A, SLAB_B_B, SLAB_B_G, SLAB_C_Q, SLAB_C_G = range(8)
SLAB_MERGE = 8

LANES = 128
SUBLANES = 8
ROWS = 16
MXU_COLS = 256
HALO = 32
CONV_ROWS = 64
CONV_IN_ROWS = CONV_ROWS + HALO
TILE_TOKENS = 256
VMEM_LIMIT_BYTES = 58 * 1024 * 1024

_BF16 = jnp.bfloat16
_F32 = jnp.float32

_GELU_C0 = 0.7978845608028654
_GELU_C1 = _GELU_C0 * 0.044715


def _gelu_tanh(x):
    hx = 0.5 * x
    return hx + hx * jnp.tanh(x * (_GELU_C0 + _GELU_C1 * (x * x)))


def _sigmoid(x):
    return 0.5 + 0.5 * jnp.tanh(0.5 * x)


def _silu(x):
    hx = 0.5 * x
    return hx + hx * jnp.tanh(hx)


def _layer_norm_rows(x, g, b):
    mu = jnp.mean(x, axis=-1, keepdims=True)
    xc = x - mu
    var = jnp.mean(xc * xc, axis=-1, keepdims=True)
    return xc * lax.rsqrt(var + LN_EPS) * g + b


def _rms_norm_rows(x, g):
    return x * lax.rsqrt(jnp.mean(x * x, axis=-1, keepdims=True) + RMS_EPS) * g


def _dot(a, b):
    return jnp.dot(a, b, preferred_element_type=_F32)


def _kv_kernel(mem_ref, g_ref, wkv_ref, k_ref, v_ref):
    mem_n = _rms_norm_rows(mem_ref[...], g_ref[...])
    kv = _dot(mem_n.astype(_BF16), wkv_ref[...])
    k_ref[...] = (kv[:, :WIDTH] * (HEAD_DIM ** -0.5)).astype(_BF16)
    v_ref[...] = kv[:, WIDTH:].astype(_BF16)


def _keys_values(mem, mem_norm_g, w_kv_bf16):
    depth = w_kv_bf16.shape[0]
    batch = mem.shape[0]
    out = jax.ShapeDtypeStruct((depth, batch, MEM_LEN, WIDTH), _BF16)
    kv_spec = pl.BlockSpec((None, None, MEM_LEN, WIDTH), lambda l, b: (l, b, 0, 0))
    return pl.pallas_call(
        _kv_kernel,
        grid=(depth, batch),
        in_specs=[
            pl.BlockSpec((None, MEM_LEN, D_MODEL), lambda l, b: (b, 0, 0)),
            pl.BlockSpec((None, 1, D_MODEL), lambda l, b: (l, 0, 0)),
            pl.BlockSpec((None, D_MODEL, 2 * WIDTH), lambda l, b: (l, 0, 0)),
        ],
        out_specs=[kv_spec, kv_spec],
        out_shape=[out, out],
        compiler_params=pltpu.CompilerParams(dimension_semantics=("arbitrary", "arbitrary")),
        name="keys_values",
    )(mem, mem_norm_g.reshape(depth, 1, D_MODEL), w_kv_bf16)


class _Task:
    def __init__(self, cost, emit, deps=()):
        self.cost, self.emit, self.deps, self.finish = cost, emit, tuple(deps), None


def _trace_two_streams(streams):
    pending = [list(s) for s in streams]
    unit_free = [0.0, 0.0]
    while any(pending):
        best = None
        for u, tasks in enumerate(pending):
            for task in tasks:
                if all(d.finish is not None for d in task.deps):
                    start = max([unit_free[u]] + [d.finish for d in task.deps])
                    if best is None or start < best[0]:
                        best = (start, u, task)
                    if start <= unit_free[u]:
                        break
        assert best is not None, "task graph has a cycle"
        start, u, task = best
        task.emit()
        task.finish = start + task.cost
        unit_free[u] = task.finish
        pending[u].remove(task)


def _layer_kernel(x_ref, k_ref, v_ref, ng_ref, win_ref, lag_ref, lab_ref, ws_ref, bs_ref,
                  cw_ref, cb_ref, lbg_ref, lbb_ref, wb_ref, wo_ref, fg_ref,
                  o_ref,
                  h_s, v_s, bra_s, brb_s,
                  z_au, z_av, z_ag, z_ba, z_bb, z_bg, z_m0, z_m1, z_m2,
                  y_s, sv_s, pa_s, pb_s, acc_s, cbuf, *, tm, final):
    scale = tm / 256.0
    mxu, vpu = [], []
    row_blocks = range(0, tm, ROWS)
    col_chunks = range(0, WIDTH, MXU_COLS)

    def rows(ref, r):
        return ref[r:r + ROWS, :]

    def add(stream, cost, emit, deps=()):
        task = _Task(cost, emit, deps)
        stream.append(task)
        return task

    def matmul_chunks(dst, lhs_ref, rhs, deps):
        def emit(n):
            dst[:, n:n + MXU_COLS] = _dot(lhs_ref[...], rhs(n))
        return [add(mxu, 256 * scale, functools.partial(emit, n), deps) for n in col_chunks]

    def row_phase(cost, body, deps):
        return [add(vpu, cost, functools.partial(body, r), deps) for r in row_blocks]

    @pl.when(pl.program_id(1) == 0)
    def _():
        cbuf[0:HALO, :] = jnp.zeros((HALO, WIDTH), _F32)

    def rms_body(r):
        h_s[r:r + ROWS, :] = _rms_norm_rows(rows(x_ref, r), ng_ref[...]).astype(_BF16)
    t_rms = row_phase(30, rms_body, ())

    def slab(dst, j):
        return matmul_chunks(dst, h_s, lambda n: win_ref[:, j * WIDTH + n:j * WIDTH + n + MXU_COLS], t_rms)

    t_ba, t_bb = slab(z_ba, SLAB_B_A), slab(z_bb, SLAB_B_B)

    def glu_body(r):
        cbuf[HALO + r:HALO + r + ROWS, :] = rows(z_ba, r) * _sigmoid(rows(z_bb, r))
    t_glu = row_phase(25, glu_body, t_ba + t_bb)

    t_av = slab(z_av, SLAB_A_V)

    def v_body(r):
        v_s[r:r + ROWS, :] = _layer_norm_rows(_gelu_tanh(rows(z_av, r)), lag_ref[...], lab_ref[...]).astype(_BF16)
    t_v = row_phase(90, v_body, t_av)

    def spatial_body(g):
        tri = (lax.broadcasted_iota(jnp.int32, (CHUNK, CHUNK), 0)
               >= lax.broadcasted_iota(jnp.int32, (CHUNK, CHUNK), 1))
        w_g = jnp.where(tri, ws_ref[g], jnp.zeros((), _BF16))
        cols = slice(g * GROUP_DIM, (g + 1) * GROUP_DIM)
        for c in range(0, tm, CHUNK):
            sv_s[c:c + CHUNK, cols] = _dot(w_g, v_s[c:c + CHUNK, cols])
    t_sp = [add(mxu, 50 * (tm // CHUNK), functools.partial(spatial_body, g), t_v) for g in range(GROUPS)]

    t_au, t_ag = slab(z_au, SLAB_A_U), slab(z_ag, SLAB_A_G)

    def bra_body(r):
        rc = r % CHUNK
        sv = rows(sv_s, r) + bs_ref[rc:rc + ROWS, :]
        bra_s[r:r + ROWS, :] = (_gelu_tanh(rows(z_au, r)) * sv * _silu(rows(z_ag, r))).astype(_BF16)
    t_bra = row_phase(80, bra_body, t_sp + t_au + t_ag)

    first = HALO - (CONV_WIDTH - 1)

    def conv_body(r, j):
        xb = cbuf[r:r + CONV_IN_ROWS, j:j + LANES]
        y = None
        for b in range(SUBLANES):
            xs = xb if b == 0 else pltpu.roll(xb, CONV_IN_ROWS - b, axis=0)
            for k in range(CONV_WIDTH):
                off = first + k
                if off % SUBLANES != b:
                    continue
                a = off - b
                term = xs[a:a + CONV_ROWS, :] * cw_ref[k:k + 1, j:j + LANES]
                y = term if y is None else y + term
        y_s[r:r + CONV_ROWS, j:j + LANES] = y
    t_conv = [add(vpu, 170, functools.partial(conv_body, r, j), t_glu)
              for r in range(0, tm, CONV_ROWS) for j in range(0, WIDTH, LANES)]

    def carry_body():
        cbuf[0:HALO, :] = cbuf[tm:tm + HALO, :]
    t_carry = [add(vpu, 10, carry_body, t_conv)]

    t_bg = slab(z_bg, SLAB_B_G)

    def brb_body(r):
        c = _silu(_layer_norm_rows(rows(y_s, r) + cb_ref[...], lbg_ref[...], lbb_ref[...]))
        brb_s[r:r + ROWS, :] = (c * _silu(rows(z_bg, r))).astype(_BF16)
    t_brb = row_phase(90, brb_body, t_conv + t_bg)

    t_m0, t_m1 = slab(z_m0, SLAB_MERGE + 0), slab(z_m1, SLAB_MERGE + 1)
    t_pa = matmul_chunks(pa_s, bra_s, lambda n: wb_ref[0, :, n:n + MXU_COLS], t_bra)
    t_pb = matmul_chunks(pb_s, brb_s, lambda n: wb_ref[1, :, n:n + MXU_COLS], t_brb)

    def acc_a_body(r):
        acc_s[r:r + ROWS, :] = _sigmoid(rows(z_m0, r)) * rows(pa_s, r)
    t_acc_a = row_phase(25, acc_a_body, t_m0 + t_pa)

    def acc_b_body(r):
        acc_s[r:r + ROWS, :] = rows(acc_s, r) + _sigmoid(rows(z_m1, r)) * rows(pb_s, r)
    t_acc_b = row_phase(25, acc_b_body, t_m1 + t_pb + t_acc_a)

    z_cq, z_cg = z_ba, z_bb
    t_cq = matmul_chunks(z_cq, h_s, lambda n: win_ref[:, SLAB_C_Q * WIDTH + n:SLAB_C_Q * WIDTH + n + MXU_COLS],
                         t_rms + t_glu)
    nt_dims = (((1,), (1,)), ((), ()))

    def scores_body(hd):
        cols = slice(hd * HEAD_DIM, (hd + 1) * HEAD_DIM)
        sv_s[:, hd * MEM_LEN:(hd + 1) * MEM_LEN] = lax.dot_general(
            z_cq[:, cols].astype(_BF16), k_ref[:, cols], nt_dims, preferred_element_type=_F32)
    t_sc = [add(mxu, 100 * scale, functools.partial(scores_body, hd), t_cq + t_bra) for hd in range(HEADS)]

    def softmax_body(r):
        for hd in range(HEADS):
            cols = slice(hd * MEM_LEN, (hd + 1) * MEM_LEN)
            s = sv_s[r:r + ROWS, cols]
            p = jnp.exp(s - jnp.max(s, axis=-1, keepdims=True))
            v_s[r:r + ROWS, cols] = (p * (1.0 / jnp.sum(p, axis=-1, keepdims=True))).astype(_BF16)
    t_sm = row_phase(40, softmax_body, t_sc + t_sp)

    def pv_body(hd):
        cols = slice(hd * HEAD_DIM, (hd + 1) * HEAD_DIM)
        y_s[:, cols] = _dot(v_s[:, hd * MEM_LEN:(hd + 1) * MEM_LEN], v_ref[:, cols])
    t_pv = [add(mxu, 100 * scale, functools.partial(pv_body, hd), t_sm + t_brb) for hd in range(HEADS)]

    t_cg = matmul_chunks(z_cg, h_s, lambda n: win_ref[:, SLAB_C_G * WIDTH + n:SLAB_C_G * WIDTH + n + MXU_COLS],
                         t_rms + t_glu)
    t_m2 = slab(z_m2, SLAB_MERGE + 2)

    def brc_body(r):
        bra_s[r:r + ROWS, :] = (rows(y_s, r) * _silu(rows(z_cg, r))).astype(_BF16)
    t_brc = row_phase(25, brc_body, t_pv + t_cg + t_pa)

    t_pc = matmul_chunks(pa_s, bra_s, lambda n: wb_ref[2, :, n:n + MXU_COLS], t_brc + t_acc_a)

    def acc_c_body(r):
        merged = rows(acc_s, r) + _sigmoid(rows(z_m2, r)) * rows(pa_s, r)
        brb_s[r:r + ROWS, :] = merged.astype(_BF16)
    t_acc_c = row_phase(25, acc_c_body, t_m2 + t_pc + t_acc_b + t_pb)

    t_po = matmul_chunks(pb_s, brb_s, lambda n: wo_ref[:, n:n + MXU_COLS], t_acc_c)

    def out_body(r):
        y = rows(x_ref, r) + rows(pb_s, r)
        if final:
            y = _rms_norm_rows(y, fg_ref[...])
        o_ref[r:r + ROWS, :] = y
    row_phase(35 if final else 10, out_body, t_po)

    mxu_order = (t_sp + t_pa + t_pb + t_sc + t_pv + t_pc + t_po
                 + t_ba + t_bb + t_av + t_au + t_ag + t_cq + t_bg + t_m0 + t_cg + t_m1 + t_m2)
    assert len(mxu_order) == len(mxu)
    del t_carry
    _trace_two_streams((mxu_order, vpu))


def _layer(x, k_all, v_all, layer, final, norm_g, w_in, lag, lab, w_s, bs_full, conv_w, conv_b,
           lbg, lbb, w_branch, w_out, final_g):
    batch, seq, _ = x.shape
    tm = TILE_TOKENS
    assert seq % tm == 0 and tm % CHUNK == 0 and tm % CONV_ROWS == 0

    def resident(shape):
        nd = len(shape)
        return pl.BlockSpec((None,) + shape, lambda b, s: (layer,) + (0,) * nd,
                            pipeline_mode=pl.Buffered(1))

    tile = pl.BlockSpec((None, tm, D_MODEL), lambda b, s: (b, s, 0))
    kv_spec = pl.BlockSpec((None, None, MEM_LEN, WIDTH), lambda b, s: (layer, b, 0, 0))
    row = (1, WIDTH)
    f32_tile = pltpu.VMEM((tm, WIDTH), _F32)
    bf16_tile = pltpu.VMEM((tm, WIDTH), _BF16)
    return pl.pallas_call(
        functools.partial(_layer_kernel, tm=tm, final=final),
        grid=(batch, seq // tm),
        in_specs=[
            tile, kv_spec, kv_spec,
            resident(row),
            resident((D_MODEL, N_IN)),
            resident(row), resident(row),
            resident((GROUPS, CHUNK, CHUNK)),
            resident((CHUNK, WIDTH)),
            resident((CONV_WIDTH, WIDTH)),
            resident(row), resident(row), resident(row),
            resident((3, WIDTH, D_MODEL)),
            resident((D_MODEL, D_MODEL)),
            pl.BlockSpec((1, D_MODEL), lambda b, s: (0, 0), pipeline_mode=pl.Buffered(1)),
        ],
        out_specs=tile,
        out_shape=jax.ShapeDtypeStruct(x.shape, x.dtype),
        scratch_shapes=[bf16_tile] * 4 + [f32_tile] * 14 + [pltpu.VMEM((tm + HALO, WIDTH), _F32)],
        compiler_params=pltpu.CompilerParams(
            dimension_semantics=("arbitrary", "arbitrary"),
            vmem_limit_bytes=VMEM_LIMIT_BYTES),
        name=f"trunk_layer_{layer}",
    )(x, k_all, v_all, norm_g, w_in, lag, lab, w_s, bs_full, conv_w, conv_b, lbg, lbb,
      w_branch, w_out, final_g)


def kernel(x, mem, norm_g, mem_norm_g, w_in, gmlp_ln_g, gmlp_ln_b, w_s, b_s, conv_w, conv_b,
           conv_ln_g, conv_ln_b, w_kv, w_branch, w_out, final_norm_g):
    depth = w_in.shape[0]
    vec = lambda a: a.reshape(depth, 1, a.shape[-1])
    bs_full = jnp.repeat(jnp.swapaxes(b_s, 1, 2), GROUP_DIM, axis=2)
    k_all, v_all = _keys_values(mem, mem_norm_g, w_kv.astype(_BF16))
    w_in_b, w_s_b = w_in.astype(_BF16), w_s.astype(_BF16)
    w_branch_b, w_out_b = w_branch.astype(_BF16), w_out.astype(_BF16)
    final_g = final_norm_g.reshape(1, D_MODEL)
    for layer in range(depth):
        x = _layer(x, k_all, v_all, layer, layer == depth - 1, vec(norm_g), w_in_b,
                   vec(gmlp_ln_g), vec(gmlp_ln_b), w_s_b, bs_full, conv_w, vec(conv_b),
                   vec(conv_ln_g), vec(conv_ln_b), w_branch_b, w_out_b, final_g)
    return x
```

```python
import functools

import jax
import jax.numpy as jnp
from jax import lax
from jax.experimental import pallas as pl
from jax.experimental.pallas import tpu as pltpu

D_MODEL = 1024
WIDTH = 1024
CHUNK = 128
GROUPS = 8
GROUP_DIM = WIDTH // GROUPS
CONV_WIDTH = 31
HEADS = 4
HEAD_DIM = WIDTH // HEADS
MEM_LEN = 256
N_SLABS = 11
N_IN = N_SLABS * WIDTH
RMS_EPS = 1e-6
LN_EPS = 1e-5

SLAB_A_U, SLAB_A_V, SLAB_A_G, SLAB_B_A, SLAB_B_B, SLAB_B_G, SLAB_C_Q, SLAB_C_G = range(8)
SLAB_MERGE = 8

LANES = 128
SUBLANES = 8
ROWS = 16
MXU_COLS = 512
CHUNKS_PER_SLAB = WIDTH // MXU_COLS
HALO = 32
CONV_ROWS = 64
CONV_IN_ROWS = CONV_ROWS + HALO
TILE_TOKENS = 256
VMEM_LIMIT_BYTES = 58 * 1024 * 1024

_BF16 = jnp.bfloat16
_F32 = jnp.float32

_GELU_C0 = 0.7978845608028654
_GELU_C1 = _GELU_C0 * 0.044715


def _gelu_tanh(x):
    hx = 0.5 * x
    return hx + hx * jnp.tanh(x * (_GELU_C0 + _GELU_C1 * (x * x)))


def _sigmoid(x):
    return 0.5 + 0.5 * jnp.tanh(0.5 * x)


def _silu(x):
    hx = 0.5 * x
    return hx + hx * jnp.tanh(hx)


def _layer_norm_rows(x, g, b):
    mu = jnp.mean(x, axis=-1, keepdims=True)
    xc = x - mu
    var = jnp.mean(xc * xc, axis=-1, keepdims=True)
    return xc * lax.rsqrt(var + LN_EPS) * g + b


def _rms_norm_rows(x, g):
    return x * lax.rsqrt(jnp.mean(x * x, axis=-1, keepdims=True) + RMS_EPS) * g


def _dot(a, b):
    return jnp.dot(a, b, preferred_element_type=_F32)


def _kv_kernel(mem_ref, g_ref, wkv_ref, k_ref, v_ref):
    mem_n = _rms_norm_rows(mem_ref[...], g_ref[...])
    kv = _dot(mem_n.astype(_BF16), wkv_ref[...])
    k_ref[...] = (kv[:, :WIDTH] * (HEAD_DIM ** -0.5)).astype(_BF16)
    v_ref[...] = kv[:, WIDTH:].astype(_BF16)


def _keys_values(mem, mem_norm_g, w_kv_bf16):
    depth = w_kv_bf16.shape[0]
    batch = mem.shape[0]
    out = jax.ShapeDtypeStruct((depth, batch, MEM_LEN, WIDTH), _BF16)
    kv_spec = pl.BlockSpec((None, None, MEM_LEN, WIDTH), lambda l, b: (l, b, 0, 0))
    return pl.pallas_call(
        _kv_kernel,
        grid=(depth, batch),
        in_specs=[
            pl.BlockSpec((None, MEM_LEN, D_MODEL), lambda l, b: (b, 0, 0)),
            pl.BlockSpec((None, 1, D_MODEL), lambda l, b: (l, 0, 0)),
            pl.BlockSpec((None, D_MODEL, 2 * WIDTH), lambda l, b: (l, 0, 0)),
        ],
        out_specs=[kv_spec, kv_spec],
        out_shape=[out, out],
        compiler_params=pltpu.CompilerParams(dimension_semantics=("arbitrary", "arbitrary")),
        name="keys_values",
    )(mem, mem_norm_g.reshape(depth, 1, D_MODEL), w_kv_bf16)


class _Task:
    def __init__(self, cost, emit, deps=()):
        self.cost, self.emit, self.deps, self.finish = cost, emit, tuple(deps), None


def _trace_two_streams(streams):
    pending = [list(s) for s in streams]
    unit_free = [0.0, 0.0]
    while any(pending):
        best = None
        for u, tasks in enumerate(pending):
            for task in tasks:
                if all(d.finish is not None for d in task.deps):
                    start = max([unit_free[u]] + [d.finish for d in task.deps])
                    if best is None or start < best[0]:
                        best = (start, u, task)
                    if start <= unit_free[u]:
                        break
        assert best is not None, "task graph has a cycle"
        start, u, task = best
        task.emit()
        task.finish = start + task.cost
        unit_free[u] = task.finish
        pending[u].remove(task)


def _layer_kernel(x_ref, k_ref, v_ref, ng_ref, win_ref, lag_ref, lab_ref, ws_ref, bs_ref,
                  cw_ref, cb_ref, lbg_ref, lbb_ref, wb_ref, wo_ref, fg_ref,
                  o_ref,
                  h_s, v_s, bra_s, brb_s,
                  z_au, z_av, z_ag, z_ba, z_bb, z_bg, z_m0, z_m1, z_m2,
                  y_s, sv_s, pa_s, pb_s, acc_s, cbuf, *, tm, final):
    scale = tm / 256.0
    mxu, vpu = [], []
    row_blocks = range(0, tm, ROWS)

    def rows(ref, r):
        return ref[r:r + ROWS, :]

    def add(stream, cost, emit, deps=()):
        task = _Task(cost, emit, deps)
        stream.append(task)
        return task

    def matmul_chunks(dst, lhs_ref, rhs, deps):
        def emit(c):
            dst[:, c * MXU_COLS:(c + 1) * MXU_COLS] = _dot(lhs_ref[...], rhs(c))
        return [add(mxu, 512 * scale, functools.partial(emit, c), deps) for c in range(CHUNKS_PER_SLAB)]

    def row_phase(cost, body, deps):
        return [add(vpu, cost, functools.partial(body, r), deps) for r in row_blocks]

    @pl.when(pl.program_id(1) == 0)
    def _():
        cbuf[0:HALO, :] = jnp.zeros((HALO, WIDTH), _F32)

    def rms_body(r):
        h_s[r:r + ROWS, :] = _rms_norm_rows(rows(x_ref, r), ng_ref[...]).astype(_BF16)
    t_rms = row_phase(30, rms_body, ())

    def slab(dst, j):
        return matmul_chunks(dst, h_s, lambda c: win_ref[j * CHUNKS_PER_SLAB + c], t_rms)

    t_ba, t_bb = slab(z_ba, SLAB_B_A), slab(z_bb, SLAB_B_B)

    def glu_body(r):
        cbuf[HALO + r:HALO + r + ROWS, :] = rows(z_ba, r) * _sigmoid(rows(z_bb, r))
    t_glu = row_phase(25, glu_body, t_ba + t_bb)

    t_av = slab(z_av, SLAB_A_V)

    def v_body(r):
        v_s[r:r + ROWS, :] = _layer_norm_rows(_gelu_tanh(rows(z_av, r)), lag_ref[...], lab_ref[...]).astype(_BF16)
    t_v = row_phase(90, v_body, t_av)

    def spatial_body(g):
        tri = (lax.broadcasted_iota(jnp.int32, (CHUNK, CHUNK), 0)
               >= lax.broadcasted_iota(jnp.int32, (CHUNK, CHUNK), 1))
        w_g = jnp.where(tri, ws_ref[g], jnp.zeros((), _BF16))
        cols = slice(g * GROUP_DIM, (g + 1) * GROUP_DIM)
        for c in range(0, tm, CHUNK):
            sv_s[c:c + CHUNK, cols] = _dot(w_g, v_s[c:c + CHUNK, cols])
    t_sp = [add(mxu, 50 * (tm // CHUNK), functools.partial(spatial_body, g), t_v) for g in range(GROUPS)]

    t_au, t_ag = slab(z_au, SLAB_A_U), slab(z_ag, SLAB_A_G)

    def bra_body(r):
        rc = r % CHUNK
        sv = rows(sv_s, r) + bs_ref[rc:rc + ROWS, :]
        bra_s[r:r + ROWS, :] = (_gelu_tanh(rows(z_au, r)) * sv * _silu(rows(z_ag, r))).astype(_BF16)
    t_bra = row_phase(80, bra_body, t_sp + t_au + t_ag)

    first = HALO - (CONV_WIDTH - 1)

    def conv_body(r, j):
        xb = cbuf[r:r + CONV_IN_ROWS, j:j + LANES]
        y = None
        for b in range(SUBLANES):
            xs = xb if b == 0 else pltpu.roll(xb, CONV_IN_ROWS - b, axis=0)
            for k in range(CONV_WIDTH):
                off = first + k
                if off % SUBLANES != b:
                    continue
                a = off - b
                term = xs[a:a + CONV_ROWS, :] * cw_ref[k:k + 1, j:j + LANES]
                y = term if y is None else y + term
        y_s[r:r + CONV_ROWS, j:j + LANES] = y
    t_conv = [add(vpu, 170, functools.partial(conv_body, r, j), t_glu)
              for r in range(0, tm, CONV_ROWS) for j in range(0, WIDTH, LANES)]

    def carry_body():
        cbuf[0:HALO, :] = cbuf[tm:tm + HALO, :]
    t_carry = [add(vpu, 10, carry_body, t_conv)]

    t_bg = slab(z_bg, SLAB_B_G)

    def brb_body(r):
        c = _silu(_layer_norm_rows(rows(y_s, r) + cb_ref[...], lbg_ref[...], lbb_ref[...]))
        brb_s[r:r + ROWS, :] = (c * _silu(rows(z_bg, r))).astype(_BF16)
    t_brb = row_phase(90, brb_body, t_conv + t_bg)

    t_m0, t_m1 = slab(z_m0, SLAB_MERGE + 0), slab(z_m1, SLAB_MERGE + 1)
    t_pa = matmul_chunks(pa_s, bra_s, lambda c: wb_ref[0 * CHUNKS_PER_SLAB + c], t_bra)
    t_pb = matmul_chunks(pb_s, brb_s, lambda c: wb_ref[1 * CHUNKS_PER_SLAB + c], t_brb)

    def acc_a_body(r):
        acc_s[r:r + ROWS, :] = _sigmoid(rows(z_m0, r)) * rows(pa_s, r)
    t_acc_a = row_phase(25, acc_a_body, t_m0 + t_pa)

    def acc_b_body(r):
        acc_s[r:r + ROWS, :] = rows(acc_s, r) + _sigmoid(rows(z_m1, r)) * rows(pb_s, r)
    t_acc_b = row_phase(25, acc_b_body, t_m1 + t_pb + t_acc_a)

    z_cq, z_cg = z_ba, z_bb
    t_cq = matmul_chunks(z_cq, h_s, lambda c: win_ref[SLAB_C_Q * CHUNKS_PER_SLAB + c], t_rms + t_glu)
    nt_dims = (((1,), (1,)), ((), ()))

    def scores_body(hd):
        cols = slice(hd * HEAD_DIM, (hd + 1) * HEAD_DIM)
        sv_s[:, hd * MEM_LEN:(hd + 1) * MEM_LEN] = lax.dot_general(
            z_cq[:, cols].astype(_BF16), k_ref[:, cols], nt_dims, preferred_element_type=_F32)
    t_sc = [add(mxu, 100 * scale, functools.partial(scores_body, hd), t_cq + t_bra) for hd in range(HEADS)]

    def softmax_body(r):
        for hd in range(HEADS):
            cols = slice(hd * MEM_LEN, (hd + 1) * MEM_LEN)
            s = sv_s[r:r + ROWS, cols]
            p = jnp.exp(s - jnp.max(s, axis=-1, keepdims=True))
            v_s[r:r + ROWS, cols] = (p * (1.0 / jnp.sum(p, axis=-1, keepdims=True))).astype(_BF16)
    t_sm = row_phase(40, softmax_body, t_sc + t_sp)

    def pv_body(hd):
        cols = slice(hd * HEAD_DIM, (hd + 1) * HEAD_DIM)
        y_s[:, cols] = _dot(v_s[:, hd * MEM_LEN:(hd + 1) * MEM_LEN], v_ref[:, cols])
    t_pv = [add(mxu, 100 * scale, functools.partial(pv_body, hd), t_sm + t_brb) for hd in range(HEADS)]

    t_cg = matmul_chunks(z_cg, h_s, lambda c: win_ref[SLAB_C_G * CHUNKS_PER_SLAB + c], t_rms + t_glu)
    t_m2 = slab(z_m2, SLAB_MERGE + 2)

    def brc_body(r):
        bra_s[r:r + ROWS, :] = (rows(y_s, r) * _silu(rows(z_cg, r))).astype(_BF16)
    t_brc = row_phase(25, brc_body, t_pv + t_cg + t_pa)

    t_pc = matmul_chunks(pa_s, bra_s, lambda c: wb_ref[2 * CHUNKS_PER_SLAB + c], t_brc + t_acc_a)

    def acc_c_body(r):
        merged = rows(acc_s, r) + _sigmoid(rows(z_m2, r)) * rows(pa_s, r)
        brb_s[r:r + ROWS, :] = merged.astype(_BF16)
    t_acc_c = row_phase(25, acc_c_body, t_m2 + t_pc + t_acc_b + t_pb)

    t_po = matmul_chunks(pb_s, brb_s, lambda c: wo_ref[c], t_acc_c)

    def out_body(r):
        y = rows(x_ref, r) + rows(pb_s, r)
        if final:
            y = _rms_norm_rows(y, fg_ref[...])
        o_ref[r:r + ROWS, :] = y
    row_phase(35 if final else 10, out_body, t_po)

    mxu_order = (t_sp + t_pa + t_pb + t_sc + t_pv + t_pc + t_po
                 + t_ba + t_bb + t_av + t_au + t_ag + t_cq + t_bg + t_m0 + t_cg + t_m1 + t_m2)
    assert len(mxu_order) == len(mxu)
    del t_carry
    _trace_two_streams((mxu_order, vpu))


def _layer(x, k_all, v_all, layer, final, norm_g, w_in, lag, lab, w_s, bs_full, conv_w, conv_b,
           lbg, lbb, w_branch, w_out, final_g):
    batch, seq, _ = x.shape
    tm = TILE_TOKENS
    assert seq % tm == 0 and tm % CHUNK == 0 and tm % CONV_ROWS == 0

    def resident(shape):
        nd = len(shape)
        return pl.BlockSpec((None,) + shape, lambda b, s: (layer,) + (0,) * nd,
                            pipeline_mode=pl.Buffered(1))

    tile = pl.BlockSpec((None, tm, D_MODEL), lambda b, s: (b, s, 0))
    kv_spec = pl.BlockSpec((None, None, MEM_LEN, WIDTH), lambda b, s: (layer, b, 0, 0))
    row = (1, WIDTH)
    f32_tile = pltpu.VMEM((tm, WIDTH), _F32)
    bf16_tile = pltpu.VMEM((tm, WIDTH), _BF16)
    return pl.pallas_call(
        functools.partial(_layer_kernel, tm=tm, final=final),
        grid=(batch, seq // tm),
        in_specs=[
            tile, kv_spec, kv_spec,
            resident(row),
            resident((N_SLABS * CHUNKS_PER_SLAB, D_MODEL, MXU_COLS)),
            resident(row), resident(row),
            resident((GROUPS, CHUNK, CHUNK)),
            resident((CHUNK, WIDTH)),
            resident((CONV_WIDTH, WIDTH)),
            resident(row), resident(row), resident(row),
            resident((3 * CHUNKS_PER_SLAB, WIDTH, MXU_COLS)),
            resident((CHUNKS_PER_SLAB, D_MODEL, MXU_COLS)),
            pl.BlockSpec((1, D_MODEL), lambda b, s: (0, 0), pipeline_mode=pl.Buffered(1)),
        ],
        out_specs=tile,
        out_shape=jax.ShapeDtypeStruct(x.shape, x.dtype),
        scratch_shapes=[bf16_tile] * 4 + [f32_tile] * 14 + [pltpu.VMEM((tm + HALO, WIDTH), _F32)],
        compiler_params=pltpu.CompilerParams(
            dimension_semantics=("arbitrary", "arbitrary"),
            vmem_limit_bytes=VMEM_LIMIT_BYTES),
        name=f"trunk_layer_{layer}",
    )(x, k_all, v_all, norm_g, w_in, lag, lab, w_s, bs_full, conv_w, conv_b, lbg, lbb,
      w_branch, w_out, final_g)


def _column_chunks(w):
    *lead, k, n = w.shape
    w = w.reshape(*lead, k, n // MXU_COLS, MXU_COLS)
    return jnp.swapaxes(w, -3, -2)


def kernel(x, mem, norm_g, mem_norm_g, w_in, gmlp_ln_g, gmlp_ln_b, w_s, b_s, conv_w, conv_b,
           conv_ln_g, conv_ln_b, w_kv, w_branch, w_out, final_norm_g):
    depth = w_in.shape[0]
    vec = lambda a: a.reshape(depth, 1, a.shape[-1])
    bs_full = jnp.repeat(jnp.swapaxes(b_s, 1, 2), GROUP_DIM, axis=2)
    k_all, v_all = _keys_values(mem, mem_norm_g, w_kv.astype(_BF16))
    w_s_b = w_s.astype(_BF16)
    w_in_b = _column_chunks(w_in.astype(_BF16))
    w_branch_b = _column_chunks(w_branch.astype(_BF16)).reshape(depth, -1, WIDTH, MXU_COLS)
    w_out_b = _column_chunks(w_out.astype(_BF16))
    final_g = final_norm_g.reshape(1, D_MODEL)
    for layer in range(depth):
        x = _layer(x, k_all, v_all, layer, layer == depth - 1, vec(norm_g), w_in_b,
                   vec(gmlp_ln_g), vec(gmlp_ln_b), w_s_b, bs_full, conv_w, vec(conv_b),
                   vec(conv_ln_g), vec(conv_ln_b), w_branch_b, w_out_b, final_g)
    return x
```

```python
import functools

import jax
import jax.numpy as jnp
from jax import lax
from jax.experimental import pallas as pl
from jax.experimental.pallas import tpu as pltpu

D_MODEL = 1024
WIDTH = 1024
CHUNK = 128
GROUPS = 8
GROUP_DIM = WIDTH // GROUPS
CONV_WIDTH = 31
HEADS = 4
HEAD_DIM = WIDTH // HEADS
MEM_LEN = 256
N_SLABS = 11
N_IN = N_SLABS * WIDTH
RMS_EPS = 1e-6
LN_EPS = 1e-5

SLAB_A_U, SLAB_A_V, SLAB_A_G, SLAB_B_A, SLAB_B_B, SLAB_B_G, SLAB_C_Q, SLAB_C_G = range(8)
SLAB_MERGE = 8

LANES = 128
SUBLANES = 8
ROWS = 16
MXU_COLS = 512
CHUNKS_PER_SLAB = WIDTH // MXU_COLS
HALO = 32
CONV_ROWS = 64
TILE_TOKENS = 256
VMEM_LIMIT_BYTES = 58 * 1024 * 1024

_BF16 = jnp.bfloat16
_F32 = jnp.float32

_GELU_C0 = 0.7978845608028654
_GELU_C1 = _GELU_C0 * 0.044715


def _gelu_tanh(x):
    hx = 0.5 * x
    return hx + hx * jnp.tanh(x * (_GELU_C0 + _GELU_C1 * (x * x)))


def _sigmoid(x):
    return 0.5 + 0.5 * jnp.tanh(0.5 * x)


def _silu(x):
    hx = 0.5 * x
    return hx + hx * jnp.tanh(hx)


def _layer_norm_rows(x, g, b):
    mu = jnp.mean(x, axis=-1, keepdims=True)
    xc = x - mu
    var = jnp.mean(xc * xc, axis=-1, keepdims=True)
    return xc * lax.rsqrt(var + LN_EPS) * g + b


def _rms_norm_rows(x, g):
    return x * lax.rsqrt(jnp.mean(x * x, axis=-1, keepdims=True) + RMS_EPS) * g


def _dot(a, b):
    return jnp.dot(a, b, preferred_element_type=_F32)


def _kv_kernel(mem_ref, g_ref, wkv_ref, k_ref, v_ref):
    mem_n = _rms_norm_rows(mem_ref[...], g_ref[...])
    kv = _dot(mem_n.astype(_BF16), wkv_ref[...])
    k_ref[...] = (kv[:, :WIDTH] * (HEAD_DIM ** -0.5)).astype(_BF16)
    v_ref[...] = kv[:, WIDTH:].astype(_BF16)


def _keys_values(mem, mem_norm_g, w_kv_bf16):
    depth = w_kv_bf16.shape[0]
    batch = mem.shape[0]
    out = jax.ShapeDtypeStruct((depth, batch, MEM_LEN, WIDTH), _BF16)
    kv_spec = pl.BlockSpec((None, None, MEM_LEN, WIDTH), lambda l, b: (l, b, 0, 0))
    return pl.pallas_call(
        _kv_kernel,
        grid=(depth, batch),
        in_specs=[
            pl.BlockSpec((None, MEM_LEN, D_MODEL), lambda l, b: (b, 0, 0)),
            pl.BlockSpec((None, 1, D_MODEL), lambda l, b: (l, 0, 0)),
            pl.BlockSpec((None, D_MODEL, 2 * WIDTH), lambda l, b: (l, 0, 0)),
        ],
        out_specs=[kv_spec, kv_spec],
        out_shape=[out, out],
        compiler_params=pltpu.CompilerParams(dimension_semantics=("arbitrary", "arbitrary")),
        name="keys_values",
    )(mem, mem_norm_g.reshape(depth, 1, D_MODEL), w_kv_bf16)


class _Task:
    def __init__(self, cost, emit, deps=()):
        self.cost, self.emit, self.deps, self.finish = cost, emit, tuple(deps), None


def _trace_two_streams(streams):
    pending = [list(s) for s in streams]
    unit_free = [0.0, 0.0]
    while any(pending):
        best = None
        for u, tasks in enumerate(pending):
            for task in tasks:
                if all(d.finish is not None for d in task.deps):
                    start = max([unit_free[u]] + [d.finish for d in task.deps])
                    if best is None or start < best[0]:
                        best = (start, u, task)
                    if start <= unit_free[u]:
                        break
        assert best is not None, "task graph has a cycle"
        start, u, task = best
        task.emit()
        task.finish = start + task.cost
        unit_free[u] = task.finish
        pending[u].remove(task)


def _layer_kernel(x_ref, k_ref, v_ref, ng_ref, win_ref, lag_ref, lab_ref, ws_ref, bs_ref,
                  cw_ref, cb_ref, lbg_ref, lbb_ref, wb_ref, wo_ref, fg_ref,
                  o_ref,
                  h_s, v_s, bra_s, brb_s, q_s,
                  z_au, z_av, z_ag, z_bg, z_cg, z_m0, z_m1, z_m2,
                  y_s, sv_s, acc_s, cbuf, *, tm, final):
    scale = tm / 256.0
    mxu, vpu = [], []
    row_blocks = range(0, tm, ROWS)

    def rows(ref, r):
        return ref[r:r + ROWS, :]

    def add(stream, cost, emit, deps=()):
        task = _Task(cost, emit, deps)
        stream.append(task)
        return task

    def chunk_cols(c):
        return slice(c * MXU_COLS, (c + 1) * MXU_COLS)

    def matmul_chunks(lhs_ref, rhs, deps, finish):
        def emit(c):
            finish(c, _dot(lhs_ref[...], rhs(c)))
        return [add(mxu, 512 * scale, functools.partial(emit, c), deps) for c in range(CHUNKS_PER_SLAB)]

    def row_phase(cost, body, deps):
        return [add(vpu, cost, functools.partial(body, r), deps) for r in row_blocks]

    @pl.when(pl.program_id(1) == 0)
    def _():
        cbuf[0:HALO, :] = jnp.zeros((HALO, WIDTH), _F32)

    def rms_body(r):
        h_s[r:r + ROWS, :] = _rms_norm_rows(rows(x_ref, r), ng_ref[...]).astype(_BF16)
    t_rms = row_phase(30, rms_body, ())

    def w_in_chunk(j):
        return lambda c: win_ref[j * CHUNKS_PER_SLAB + c]

    def slab(dst, j, act):
        def finish(c, z):
            dst[:, chunk_cols(c)] = act(z).astype(dst.dtype)
        return matmul_chunks(h_s, w_in_chunk(j), t_rms, finish)

    def glu_emit(c):
        a = _dot(h_s[...], win_ref[SLAB_B_A * CHUNKS_PER_SLAB + c])
        b = _dot(h_s[...], win_ref[SLAB_B_B * CHUNKS_PER_SLAB + c])
        cbuf[HALO:HALO + tm, chunk_cols(c)] = a * _sigmoid(b)
    t_glu = [add(mxu, 1024 * scale, functools.partial(glu_emit, c), t_rms) for c in range(CHUNKS_PER_SLAB)]

    t_av = slab(z_av, SLAB_A_V, _gelu_tanh)

    def v_body(r):
        v_s[r:r + ROWS, :] = _layer_norm_rows(rows(z_av, r), lag_ref[...], lab_ref[...]).astype(_BF16)
    t_v = row_phase(50, v_body, t_av)

    def spatial_body(g):
        tri = (lax.broadcasted_iota(jnp.int32, (CHUNK, CHUNK), 0)
               >= lax.broadcasted_iota(jnp.int32, (CHUNK, CHUNK), 1))
        w_g = jnp.where(tri, ws_ref[g], jnp.zeros((), _BF16))
        cols = slice(g * GROUP_DIM, (g + 1) * GROUP_DIM)
        for c in range(0, tm, CHUNK):
            sv_s[c:c + CHUNK, cols] = _dot(w_g, v_s[c:c + CHUNK, cols])
    t_sp = [add(mxu, 50 * (tm // CHUNK), functools.partial(spatial_body, g), t_v) for g in range(GROUPS)]

    t_au, t_ag = slab(z_au, SLAB_A_U, _gelu_tanh), slab(z_ag, SLAB_A_G, _silu)

    def bra_body(r):
        rc = r % CHUNK
        sv = rows(sv_s, r) + bs_ref[rc:rc + ROWS, :]
        bra_s[r:r + ROWS, :] = (rows(z_au, r) * sv * rows(z_ag, r)).astype(_BF16)
    t_bra = row_phase(30, bra_body, t_sp + t_au + t_ag)

    first = HALO - (CONV_WIDTH - 1)

    def conv_body(r, j):
        y = None
        for b in range(SUBLANES):
            n = CONV_ROWS + (SUBLANES if b else 0)
            u = None
            for k in range(CONV_WIDTH):
                off = first + k
                if off % SUBLANES != b:
                    continue
                a = off - b
                term = cbuf[r + a:r + a + n, j:j + LANES] * cw_ref[k:k + 1, j:j + LANES]
                u = term if u is None else u + term
            if b:
                u = pltpu.roll(u, n - b, axis=0)[0:CONV_ROWS, :]
            y = u if y is None else y + u
        y_s[r:r + CONV_ROWS, j:j + LANES] = y
    t_conv = [add(vpu, 170, functools.partial(conv_body, r, j), [t_glu[j // MXU_COLS]])
              for r in range(0, tm, CONV_ROWS) for j in range(0, WIDTH, LANES)]

    def carry_body():
        cbuf[0:HALO, :] = cbuf[tm:tm + HALO, :]
    add(vpu, 10, carry_body, t_conv)

    t_bg = slab(z_bg, SLAB_B_G, _silu)

    def brb_body(r):
        c = _silu(_layer_norm_rows(rows(y_s, r) + cb_ref[...], lbg_ref[...], lbb_ref[...]))
        brb_s[r:r + ROWS, :] = (c * rows(z_bg, r)).astype(_BF16)
    t_brb = row_phase(80, brb_body, t_conv + t_bg)

    t_m0 = slab(z_m0, SLAB_MERGE + 0, _sigmoid)
    t_m1 = slab(z_m1, SLAB_MERGE + 1, _sigmoid)
    t_m2 = slab(z_m2, SLAB_MERGE + 2, _sigmoid)

    def w_branch_chunk(n):
        return lambda c: wb_ref[n * CHUNKS_PER_SLAB + c]

    def merge_a(c, p):
        acc_s[:, chunk_cols(c)] = z_m0[:, chunk_cols(c)] * p
    t_pa = matmul_chunks(bra_s, w_branch_chunk(0), t_bra + t_m0, merge_a)

    def merge_b(c, p):
        acc_s[:, chunk_cols(c)] = acc_s[:, chunk_cols(c)] + z_m1[:, chunk_cols(c)] * p
    t_pb = matmul_chunks(brb_s, w_branch_chunk(1), t_brb + t_m1 + t_pa, merge_b)

    def q_finish(c, z):
        q_s[:, chunk_cols(c)] = z.astype(_BF16)
    t_cq = matmul_chunks(h_s, w_in_chunk(SLAB_C_Q), t_rms, q_finish)
    nt_dims = (((1,), (1,)), ((), ()))

    def scores_body(hd):
        cols = slice(hd * HEAD_DIM, (hd + 1) * HEAD_DIM)
        sv_s[:, hd * MEM_LEN:(hd + 1) * MEM_LEN] = lax.dot_general(
            q_s[:, cols], k_ref[:, cols], nt_dims, preferred_element_type=_F32)
    t_sc = [add(mxu, 100 * scale, functools.partial(scores_body, hd), t_cq + t_bra) for hd in range(HEADS)]

    def softmax_body(r):
        for hd in range(HEADS):
            cols = slice(hd * MEM_LEN, (hd + 1) * MEM_LEN)
            s = sv_s[r:r + ROWS, cols]
            p = jnp.exp(s - jnp.max(s, axis=-1, keepdims=True))
            v_s[r:r + ROWS, cols] = (p * (1.0 / jnp.sum(p, axis=-1, keepdims=True))).astype(_BF16)
    t_sm = row_phase(40, softmax_body, t_sc + t_sp)

    t_cg = slab(z_cg, SLAB_C_G, _silu)

    def pv_body(hd):
        cols = slice(hd * HEAD_DIM, (hd + 1) * HEAD_DIM)
        att = _dot(v_s[:, hd * MEM_LEN:(hd + 1) * MEM_LEN], v_ref[:, cols])
        bra_s[:, cols] = (att * z_cg[:, cols]).astype(_BF16)
    t_pv = [add(mxu, 100 * scale, functools.partial(pv_body, hd), t_sm + t_cg + t_pa) for hd in range(HEADS)]

    def merge_c(c, p):
        merged = acc_s[:, chunk_cols(c)] + z_m2[:, chunk_cols(c)] * p
        brb_s[:, chunk_cols(c)] = merged.astype(_BF16)
    t_pc = matmul_chunks(bra_s, w_branch_chunk(2), t_pv + t_m2 + t_pb, merge_c)

    if final:
        def out_finish(c, p):
            y_s[:, chunk_cols(c)] = x_ref[:, chunk_cols(c)] + p
        t_po = matmul_chunks(brb_s, lambda c: wo_ref[c], t_pc + t_brb, out_finish)

        def out_body(r):
            o_ref[r:r + ROWS, :] = _rms_norm_rows(rows(y_s, r), fg_ref[...])
        row_phase(30, out_body, t_po)
    else:
        def out_finish(c, p):
            o_ref[:, chunk_cols(c)] = x_ref[:, chunk_cols(c)] + p
        t_po = matmul_chunks(brb_s, lambda c: wo_ref[c], t_pc, out_finish)

    mxu_order = (t_sp + t_pa + t_pb + t_sc + t_pv + t_pc + t_po
                 + t_glu + t_av + t_au + t_ag + t_cq + t_bg + t_m0 + t_cg + t_m1 + t_m2)
    assert len(mxu_order) == len(mxu)
    _trace_two_streams((mxu_order, vpu))


def _layer(x, k_all, v_all, layer, final, norm_g, w_in, lag, lab, w_s, bs_full, conv_w, conv_b,
           lbg, lbb, w_branch, w_out, final_g):
    batch, seq, _ = x.shape
    tm = TILE_TOKENS
    assert seq % tm == 0 and tm % CHUNK == 0 and tm % CONV_ROWS == 0

    def resident(shape):
        nd = len(shape)
        return pl.BlockSpec((None,) + shape, lambda b, s: (layer,) + (0,) * nd,
                            pipeline_mode=pl.Buffered(1))

    tile = pl.BlockSpec((None, tm, D_MODEL), lambda b, s: (b, s, 0))
    kv_spec = pl.BlockSpec((None, None, MEM_LEN, WIDTH), lambda b, s: (layer, b, 0, 0))
    row = (1, WIDTH)
    f32_tile = pltpu.VMEM((tm, WIDTH), _F32)
    bf16_tile = pltpu.VMEM((tm, WIDTH), _BF16)
    return pl.pallas_call(
        functools.partial(_layer_kernel, tm=tm, final=final),
        grid=(batch, seq // tm),
        in_specs=[
            tile, kv_spec, kv_spec,
            resident(row),
            resident((N_SLABS * CHUNKS_PER_SLAB, D_MODEL, MXU_COLS)),
            resident(row), resident(row),
            resident((GROUPS, CHUNK, CHUNK)),
            resident((CHUNK, WIDTH)),
            resident((CONV_WIDTH, WIDTH)),
            resident(row), resident(row), resident(row),
            resident((3 * CHUNKS_PER_SLAB, WIDTH, MXU_COLS)),
            resident((CHUNKS_PER_SLAB, D_MODEL, MXU_COLS)),
            pl.BlockSpec((1, D_MODEL), lambda b, s: (0, 0), pipeline_mode=pl.Buffered(1)),
        ],
        out_specs=tile,
        out_shape=jax.ShapeDtypeStruct(x.shape, x.dtype),
        scratch_shapes=[bf16_tile] * 5 + [f32_tile] * 11 + [pltpu.VMEM((tm + HALO, WIDTH), _F32)],
        compiler_params=pltpu.CompilerParams(
            dimension_semantics=("arbitrary", "arbitrary"),
            vmem_limit_bytes=VMEM_LIMIT_BYTES),
        name=f"trunk_layer_{layer}",
    )(x, k_all, v_all, norm_g, w_in, lag, lab, w_s, bs_full, conv_w, conv_b, lbg, lbb,
      w_branch, w_out, final_g)


def _column_chunks(w):
    *lead, k, n = w.shape
    w = w.reshape(*lead, k, n // MXU_COLS, MXU_COLS)
    return jnp.swapaxes(w, -3, -2)


def kernel(x, mem, norm_g, mem_norm_g, w_in, gmlp_ln_g, gmlp_ln_b, w_s, b_s, conv_w, conv_b,
           conv_ln_g, conv_ln_b, w_kv, w_branch, w_out, final_norm_g):
    depth = w_in.shape[0]
    vec = lambda a: a.reshape(depth, 1, a.shape[-1])
    bs_full = jnp.repeat(jnp.swapaxes(b_s, 1, 2), GROUP_DIM, axis=2)
    k_all, v_all = _keys_values(mem, mem_norm_g, w_kv.astype(_BF16))
    w_s_b = w_s.astype(_BF16)
    w_in_b = _column_chunks(w_in.astype(_BF16))
    w_branch_b = _column_chunks(w_branch.astype(_BF16)).reshape(depth, -1, WIDTH, MXU_COLS)
    w_out_b = _column_chunks(w_out.astype(_BF16))
    final_g = final_norm_g.reshape(1, D_MODEL)
    for layer in range(depth):
        x = _layer(x, k_all, v_all, layer, layer == depth - 1, vec(norm_g), w_in_b,
                   vec(gmlp_ln_g), vec(gmlp_ln_b), w_s_b, bs_full, conv_w, vec(conv_b),
                   vec(conv_ln_g), vec(conv_ln_b), w_branch_b, w_out_b, final_g)
    return x
```

```python
import functools

import jax
import jax.numpy as jnp
from jax import lax
from jax.experimental import pallas as pl
from jax.experimental.pallas import tpu as pltpu

D_MODEL = 1024
WIDTH = 1024
CHUNK = 128
GROUPS = 8
GROUP_DIM = WIDTH // GROUPS
CONV_WIDTH = 31
HEADS = 4
HEAD_DIM = WIDTH // HEADS
MEM_LEN = 256
N_SLABS = 11
N_IN = N_SLABS * WIDTH
RMS_EPS = 1e-6
LN_EPS = 1e-5

SLAB_A_U, SLAB_A_V, SLAB_A_G, SLAB_B_A, SLAB_B_B, SLAB_B_G, SLAB_C_Q, SLAB_C_G = range(8)
SLAB_MERGE = 8

LANES = 128
SUBLANES = 8
ROWS = 16
MXU_COLS = 512
CHUNKS_PER_SLAB = WIDTH // MXU_COLS
HALO = 32
CONV_ROWS = 64
TILE_TOKENS = 256
VMEM_LIMIT_BYTES = 58 * 1024 * 1024

_BF16 = jnp.bfloat16
_F32 = jnp.float32

_GELU_C0 = 0.7978845608028654
_GELU_C1 = _GELU_C0 * 0.044715


def _gelu_tanh(x):
    hx = 0.5 * x
    return hx + hx * jnp.tanh(x * (_GELU_C0 + _GELU_C1 * (x * x)))


def _sigmoid(x):
    return 0.5 + 0.5 * jnp.tanh(0.5 * x)


def _silu(x):
    hx = 0.5 * x
    return hx + hx * jnp.tanh(hx)


def _layer_norm_rows(x, g, b):
    mu = jnp.mean(x, axis=-1, keepdims=True)
    xc = x - mu
    var = jnp.mean(xc * xc, axis=-1, keepdims=True)
    return xc * lax.rsqrt(var + LN_EPS) * g + b


def _rms_norm_rows(x, g):
    return x * lax.rsqrt(jnp.mean(x * x, axis=-1, keepdims=True) + RMS_EPS) * g


def _dot(a, b):
    return jnp.dot(a, b, preferred_element_type=_F32)


def _kv_kernel(mem_ref, g_ref, wkv_ref, k_ref, v_ref):
    mem_n = _rms_norm_rows(mem_ref[...], g_ref[...])
    kv = _dot(mem_n.astype(_BF16), wkv_ref[...])
    k_ref[...] = (kv[:, :WIDTH] * (HEAD_DIM ** -0.5)).astype(_BF16)
    v_ref[...] = kv[:, WIDTH:].astype(_BF16)


def _keys_values(mem, mem_norm_g, w_kv_bf16):
    depth = w_kv_bf16.shape[0]
    batch = mem.shape[0]
    out = jax.ShapeDtypeStruct((depth, batch, MEM_LEN, WIDTH), _BF16)
    kv_spec = pl.BlockSpec((None, None, MEM_LEN, WIDTH), lambda l, b: (l, b, 0, 0))
    return pl.pallas_call(
        _kv_kernel,
        grid=(depth, batch),
        in_specs=[
            pl.BlockSpec((None, MEM_LEN, D_MODEL), lambda l, b: (b, 0, 0)),
            pl.BlockSpec((None, 1, D_MODEL), lambda l, b: (l, 0, 0)),
            pl.BlockSpec((None, D_MODEL, 2 * WIDTH), lambda l, b: (l, 0, 0)),
        ],
        out_specs=[kv_spec, kv_spec],
        out_shape=[out, out],
        compiler_params=pltpu.CompilerParams(dimension_semantics=("arbitrary", "arbitrary")),
        name="keys_values",
    )(mem, mem_norm_g.reshape(depth, 1, D_MODEL), w_kv_bf16)


class _Task:
    def __init__(self, cost, emit, deps=()):
        self.cost, self.emit, self.deps, self.finish = cost, emit, tuple(deps), None


def _trace_two_streams(streams):
    pending = [list(s) for s in streams]
    unit_free = [0.0, 0.0]
    while any(pending):
        best = None
        for u, tasks in enumerate(pending):
            for task in tasks:
                if all(d.finish is not None for d in task.deps):
                    start = max([unit_free[u]] + [d.finish for d in task.deps])
                    if best is None or start < best[0]:
                        best = (start, u, task)
                    if start <= unit_free[u]:
                        break
        assert best is not None, "task graph has a cycle"
        start, u, task = best
        task.emit()
        task.finish = start + task.cost
        unit_free[u] = task.finish
        pending[u].remove(task)


def _layer_kernel(x_ref, k_ref, v_ref, ng_ref, win_ref, lag_ref, lab_ref, ws_ref, bs_ref,
                  cw_ref, cb_ref, lbg_ref, lbb_ref, wb_ref, wo_ref, fg_ref,
                  o_ref,
                  h_s, v_s, bra_s, brb_s, q_s,
                  z_au, z_av, z_ag, z_bg, z_cg, z_m0, z_m1, z_m2,
                  y_s, sv_s, acc_s, cbuf, *, tm, final):
    scale = tm / 256.0
    mxu, vpu = [], []
    row_blocks = range(0, tm, ROWS)

    def rows(ref, r):
        return ref[r:r + ROWS, :]

    def add(stream, cost, emit, deps=()):
        task = _Task(cost, emit, deps)
        stream.append(task)
        return task

    def chunk_cols(c):
        return slice(c * MXU_COLS, (c + 1) * MXU_COLS)

    def matmul_chunks(lhs_ref, rhs, deps, finish):
        def emit(c):
            finish(c, _dot(lhs_ref[...], rhs(c)))
        return [add(mxu, 512 * scale, functools.partial(emit, c), deps) for c in range(CHUNKS_PER_SLAB)]

    def row_phase(cost, body, deps):
        return [add(vpu, cost, functools.partial(body, r), deps) for r in row_blocks]

    @pl.when(pl.program_id(1) == 0)
    def _():
        cbuf[0:HALO, :] = jnp.zeros((HALO, WIDTH), _F32)

    def rms_body(r):
        h_s[r:r + ROWS, :] = _rms_norm_rows(rows(x_ref, r), ng_ref[...]).astype(_BF16)
    t_rms = row_phase(30, rms_body, ())

    def w_in_chunk(j):
        return lambda c: win_ref[j * CHUNKS_PER_SLAB + c]

    def slab(dst, j, act):
        def finish(c, z):
            dst[:, chunk_cols(c)] = act(z).astype(dst.dtype)
        return matmul_chunks(h_s, w_in_chunk(j), t_rms, finish)

    def glu_emit(c):
        a = _dot(h_s[...], win_ref[SLAB_B_A * CHUNKS_PER_SLAB + c])
        b = _dot(h_s[...], win_ref[SLAB_B_B * CHUNKS_PER_SLAB + c])
        cbuf[HALO:HALO + tm, chunk_cols(c)] = a * _sigmoid(b)
    t_glu = [add(mxu, 1024 * scale, functools.partial(glu_emit, c), t_rms) for c in range(CHUNKS_PER_SLAB)]

    t_av = slab(z_av, SLAB_A_V, _gelu_tanh)

    def v_body(r):
        v_s[r:r + ROWS, :] = _layer_norm_rows(rows(z_av, r), lag_ref[...], lab_ref[...]).astype(_BF16)
    t_v = row_phase(50, v_body, t_av)

    def spatial_body(g):
        tri = (lax.broadcasted_iota(jnp.int32, (CHUNK, CHUNK), 0)
               >= lax.broadcasted_iota(jnp.int32, (CHUNK, CHUNK), 1))
        w_g = jnp.where(tri, ws_ref[g], jnp.zeros((), _BF16))
        cols = slice(g * GROUP_DIM, (g + 1) * GROUP_DIM)
        for c in range(0, tm, CHUNK):
            sv_s[c:c + CHUNK, cols] = _dot(w_g, v_s[c:c + CHUNK, cols])
    t_sp = [add(mxu, 50 * (tm // CHUNK), functools.partial(spatial_body, g), t_v) for g in range(GROUPS)]

    t_au, t_ag = slab(z_au, SLAB_A_U, _gelu_tanh), slab(z_ag, SLAB_A_G, _silu)

    def bra_body(r):
        rc = r % CHUNK
        sv = rows(sv_s, r) + bs_ref[rc:rc + ROWS, :]
        bra_s[r:r + ROWS, :] = (rows(z_au, r) * sv * rows(z_ag, r)).astype(_BF16)
    t_bra = row_phase(30, bra_body, t_sp + t_au + t_ag)

    first = HALO - (CONV_WIDTH - 1)

    def conv_body(r, j, pin=None):
        y = None
        for b in range(SUBLANES):
            n = CONV_ROWS + (SUBLANES if b else 0)
            u = None
            for k in range(CONV_WIDTH):
                off = first + k
                if off % SUBLANES != b:
                    continue
                a = off - b
                w_k = cw_ref[k:k + 1, j:j + LANES]
                if pin is not None and u is None and b == 0:
                    w_k = jnp.where(pl.program_id(0) < 0, pin()[0:1, :], w_k)
                term = cbuf[r + a:r + a + n, j:j + LANES] * w_k
                u = term if u is None else u + term
            if b:
                u = pltpu.roll(u, n - b, axis=0)[0:CONV_ROWS, :]
            y = u if y is None else y + u
        y_s[r:r + CONV_ROWS, j:j + LANES] = y

    t_m0 = slab(z_m0, SLAB_MERGE + 0, _sigmoid)
    t_m1 = slab(z_m1, SLAB_MERGE + 1, _sigmoid)
    t_m2 = slab(z_m2, SLAB_MERGE + 2, _sigmoid)
    t_cg = slab(z_cg, SLAB_C_G, _silu)
    pinned = [(t, functools.partial(lambda buf, c: buf[0:SUBLANES, c * MXU_COLS:c * MXU_COLS + LANES], buf, c))
              for buf, tasks in ((z_m0, t_m0), (z_m1, t_m1), (z_m2, t_m2), (z_cg, t_cg))
              for c, t in enumerate(tasks)]
    blocks = [(r, j) for r in range(0, tm, CONV_ROWS) for j in range(0, WIDTH, LANES)]
    first_pinned = len(blocks) - 2 * len(pinned) - 4
    t_conv = []
    for i, (r, j) in enumerate(blocks):
        deps, pin = [t_glu[j // MXU_COLS]], None
        if i >= first_pinned and (i - first_pinned) % 2 == 0 and (i - first_pinned) // 2 < len(pinned):
            task, pin = pinned[(i - first_pinned) // 2]
            deps.append(task)
        t_conv.append(add(vpu, 170, functools.partial(conv_body, r, j, pin), deps))

    def carry_body():
        cbuf[0:HALO, :] = cbuf[tm:tm + HALO, :]
    add(vpu, 10, carry_body, t_conv)

    t_bg = slab(z_bg, SLAB_B_G, _silu)

    def brb_body(r):
        c = _silu(_layer_norm_rows(rows(y_s, r) + cb_ref[...], lbg_ref[...], lbb_ref[...]))
        brb_s[r:r + ROWS, :] = (c * rows(z_bg, r)).astype(_BF16)
    t_brb = row_phase(80, brb_body, t_conv + t_bg)

    def w_branch_chunk(n):
        return lambda c: wb_ref[n * CHUNKS_PER_SLAB + c]

    def merge_a(c, p):
        acc_s[:, chunk_cols(c)] = z_m0[:, chunk_cols(c)] * p
    t_pa = matmul_chunks(bra_s, w_branch_chunk(0), t_bra + t_m0, merge_a)

    def merge_b(c, p):
        acc_s[:, chunk_cols(c)] = acc_s[:, chunk_cols(c)] + z_m1[:, chunk_cols(c)] * p
    t_pb = matmul_chunks(brb_s, w_branch_chunk(1), t_brb + t_m1 + t_pa, merge_b)

    def q_finish(c, z):
        q_s[:, chunk_cols(c)] = z.astype(_BF16)
    t_cq = matmul_chunks(h_s, w_in_chunk(SLAB_C_Q), t_rms, q_finish)
    nt_dims = (((1,), (1,)), ((), ()))

    def scores_body(hd):
        cols = slice(hd * HEAD_DIM, (hd + 1) * HEAD_DIM)
        sv_s[:, hd * MEM_LEN:(hd + 1) * MEM_LEN] = lax.dot_general(
            q_s[:, cols], k_ref[:, cols], nt_dims, preferred_element_type=_F32)
    t_sc = [add(mxu, 100 * scale, functools.partial(scores_body, hd), t_cq + t_bra) for hd in range(HEADS)]

    def softmax_body(r):
        for hd in range(HEADS):
            cols = slice(hd * MEM_LEN, (hd + 1) * MEM_LEN)
            s = sv_s[r:r + ROWS, cols]
            p = jnp.exp(s - jnp.max(s, axis=-1, keepdims=True))
            v_s[r:r + ROWS, cols] = (p * (1.0 / jnp.sum(p, axis=-1, keepdims=True))).astype(_BF16)
    t_sm = row_phase(40, softmax_body, t_sc + t_sp)

    def pv_body(hd):
        cols = slice(hd * HEAD_DIM, (hd + 1) * HEAD_DIM)
        att = _dot(v_s[:, hd * MEM_LEN:(hd + 1) * MEM_LEN], v_ref[:, cols])
        bra_s[:, cols] = (att * z_cg[:, cols]).astype(_BF16)
    t_pv = [add(mxu, 100 * scale, functools.partial(pv_body, hd), t_sm + t_cg + t_pa) for hd in range(HEADS)]

    def merge_c(c, p):
        merged = acc_s[:, chunk_cols(c)] + z_m2[:, chunk_cols(c)] * p
        brb_s[:, chunk_cols(c)] = merged.astype(_BF16)
    t_pc = matmul_chunks(bra_s, w_branch_chunk(2), t_pv + t_m2 + t_pb, merge_c)

    if final:
        def out_finish(c, p):
            y_s[:, chunk_cols(c)] = x_ref[:, chunk_cols(c)] + p
        t_po = matmul_chunks(brb_s, lambda c: wo_ref[c], t_pc + t_brb, out_finish)

        def out_body(r):
            o_ref[r:r + ROWS, :] = _rms_norm_rows(rows(y_s, r), fg_ref[...])
        row_phase(30, out_body, t_po)
    else:
        def out_finish(c, p):
            o_ref[:, chunk_cols(c)] = x_ref[:, chunk_cols(c)] + p
        t_po = matmul_chunks(brb_s, lambda c: wo_ref[c], t_pc, out_finish)

    mxu_order = (t_sp + t_pa + t_pb + t_sc + t_pv + t_pc + t_po
                 + t_glu + t_av + t_au + t_ag + t_cq + t_bg + t_m0 + t_cg + t_m1 + t_m2)
    assert len(mxu_order) == len(mxu)
    _trace_two_streams((mxu_order, vpu))


def _layer(x, k_all, v_all, layer, final, norm_g, w_in, lag, lab, w_s, bs_full, conv_w, conv_b,
           lbg, lbb, w_branch, w_out, final_g):
    batch, seq, _ = x.shape
    tm = TILE_TOKENS
    assert seq % tm == 0 and tm % CHUNK == 0 and tm % CONV_ROWS == 0

    def resident(shape):
        nd = len(shape)
        return pl.BlockSpec((None,) + shape, lambda b, s: (layer,) + (0,) * nd,
                            pipeline_mode=pl.Buffered(1))

    tile = pl.BlockSpec((None, tm, D_MODEL), lambda b, s: (b, s, 0))
    kv_spec = pl.BlockSpec((None, None, MEM_LEN, WIDTH), lambda b, s: (layer, b, 0, 0))
    row = (1, WIDTH)
    f32_tile = pltpu.VMEM((tm, WIDTH), _F32)
    bf16_tile = pltpu.VMEM((tm, WIDTH), _BF16)
    return pl.pallas_call(
        functools.partial(_layer_kernel, tm=tm, final=final),
        grid=(batch, seq // tm),
        in_specs=[
            tile, kv_spec, kv_spec,
            resident(row),
            resident((N_SLABS * CHUNKS_PER_SLAB, D_MODEL, MXU_COLS)),
            resident(row), resident(row),
            resident((GROUPS, CHUNK, CHUNK)),
            resident((CHUNK, WIDTH)),
            resident((CONV_WIDTH, WIDTH)),
            resident(row), resident(row), resident(row),
            resident((3 * CHUNKS_PER_SLAB, WIDTH, MXU_COLS)),
            resident((CHUNKS_PER_SLAB, D_MODEL, MXU_COLS)),
            pl.BlockSpec((1, D_MODEL), lambda b, s: (0, 0), pipeline_mode=pl.Buffered(1)),
        ],
        out_specs=tile,
        out_shape=jax.ShapeDtypeStruct(x.shape, x.dtype),
        scratch_shapes=[bf16_tile] * 5 + [f32_tile] * 11 + [pltpu.VMEM((tm + HALO, WIDTH), _F32)],
        compiler_params=pltpu.CompilerParams(
            dimension_semantics=("arbitrary", "arbitrary"),
            vmem_limit_bytes=VMEM_LIMIT_BYTES),
        name=f"trunk_layer_{layer}",
    )(x, k_all, v_all, norm_g, w_in, lag, lab, w_s, bs_full, conv_w, conv_b, lbg, lbb,
      w_branch, w_out, final_g)


def _column_chunks(w):
    *lead, k, n = w.shape
    w = w.reshape(*lead, k, n // MXU_COLS, MXU_COLS)
    return jnp.swapaxes(w, -3, -2)


def kernel(x, mem, norm_g, mem_norm_g, w_in, gmlp_ln_g, gmlp_ln_b, w_s, b_s, conv_w, conv_b,
           conv_ln_g, conv_ln_b, w_kv, w_branch, w_out, final_norm_g):
    depth = w_in.shape[0]
    vec = lambda a: a.reshape(depth, 1, a.shape[-1])
    bs_full = jnp.repeat(jnp.swapaxes(b_s, 1, 2), GROUP_DIM, axis=2)
    k_all, v_all = _keys_values(mem, mem_norm_g, w_kv.astype(_BF16))
    w_s_b = w_s.astype(_BF16)
    w_in_b = _column_chunks(w_in.astype(_BF16))
    w_branch_b = _column_chunks(w_branch.astype(_BF16)).reshape(depth, -1, WIDTH, MXU_COLS)
    w_out_b = _column_chunks(w_out.astype(_BF16))
    final_g = final_norm_g.reshape(1, D_MODEL)
    for layer in range(depth):
        x = _layer(x, k_all, v_all, layer, layer == depth - 1, vec(norm_g), w_in_b,
                   vec(gmlp_ln_g), vec(gmlp_ln_b), w_s_b, bs_full, conv_w, vec(conv_b),
                   vec(conv_ln_g), vec(conv_ln_b), w_branch_b, w_out_b, final_g)
    return x
```

```python
import functools

import jax
import jax.numpy as jnp
from jax import lax
from jax.experimental import pallas as pl
from jax.experimental.pallas import tpu as pltpu

D_MODEL = 1024
WIDTH = 1024
CHUNK = 128
GROUPS = 8
GROUP_DIM = WIDTH // GROUPS
CONV_WIDTH = 31
HEADS = 4
HEAD_DIM = WIDTH // HEADS
MEM_LEN = 256
N_SLABS = 11
N_IN = N_SLABS * WIDTH
RMS_EPS = 1e-6
LN_EPS = 1e-5

SLAB_A_U, SLAB_A_V, SLAB_A_G, SLAB_B_A, SLAB_B_B, SLAB_B_G, SLAB_C_Q, SLAB_C_G = range(8)
SLAB_MERGE = 8

LANES = 128
SUBLANES = 8
ROWS = 16
MXU_COLS = 512
CHUNKS_PER_SLAB = WIDTH // MXU_COLS
HALO = 32
CONV_ROWS = 64
TILE_TOKENS = 256
VMEM_LIMIT_BYTES = 58 * 1024 * 1024

_BF16 = jnp.bfloat16
_F32 = jnp.float32

_GELU_C0 = 0.7978845608028654
_GELU_C1 = _GELU_C0 * 0.044715


def _gelu_tanh(x):
    hx = 0.5 * x
    return hx + hx * jnp.tanh(x * (_GELU_C0 + _GELU_C1 * (x * x)))


def _sigmoid(x):
    return 0.5 + 0.5 * jnp.tanh(0.5 * x)


def _silu(x):
    hx = 0.5 * x
    return hx + hx * jnp.tanh(hx)


def _layer_norm_rows(x, g, b):
    mu = jnp.mean(x, axis=-1, keepdims=True)
    xc = x - mu
    var = jnp.mean(xc * xc, axis=-1, keepdims=True)
    return xc * lax.rsqrt(var + LN_EPS) * g + b


def _rms_norm_rows(x, g):
    return x * lax.rsqrt(jnp.mean(x * x, axis=-1, keepdims=True) + RMS_EPS) * g


def _dot(a, b):
    return jnp.dot(a, b, preferred_element_type=_F32)


def _kv_kernel(mem_ref, g_ref, wkv_ref, k_ref, v_ref):
    mem_n = _rms_norm_rows(mem_ref[...], g_ref[...])
    kv = _dot(mem_n.astype(_BF16), wkv_ref[...])
    k_ref[...] = (kv[:, :WIDTH] * (HEAD_DIM ** -0.5)).astype(_BF16)
    v_ref[...] = kv[:, WIDTH:].astype(_BF16)


def _keys_values(mem, mem_norm_g, w_kv_bf16):
    depth = w_kv_bf16.shape[0]
    batch = mem.shape[0]
    out = jax.ShapeDtypeStruct((depth, batch, MEM_LEN, WIDTH), _BF16)
    kv_spec = pl.BlockSpec((None, None, MEM_LEN, WIDTH), lambda l, b: (l, b, 0, 0))
    return pl.pallas_call(
        _kv_kernel,
        grid=(depth, batch),
        in_specs=[
            pl.BlockSpec((None, MEM_LEN, D_MODEL), lambda l, b: (b, 0, 0)),
            pl.BlockSpec((None, 1, D_MODEL), lambda l, b: (l, 0, 0)),
            pl.BlockSpec((None, D_MODEL, 2 * WIDTH), lambda l, b: (l, 0, 0)),
        ],
        out_specs=[kv_spec, kv_spec],
        out_shape=[out, out],
        compiler_params=pltpu.CompilerParams(dimension_semantics=("arbitrary", "arbitrary")),
        name="keys_values",
    )(mem, mem_norm_g.reshape(depth, 1, D_MODEL), w_kv_bf16)


class _Task:
    def __init__(self, cost, emit, deps=()):
        self.cost, self.emit, self.deps, self.finish = cost, emit, tuple(deps), None


def _trace_two_streams(streams):
    pending = [list(s) for s in streams]
    unit_free = [0.0, 0.0]
    while any(pending):
        best = None
        for u, tasks in enumerate(pending):
            for task in tasks:
                if all(d.finish is not None for d in task.deps):
                    start = max([unit_free[u]] + [d.finish for d in task.deps])
                    if best is None or start < best[0]:
                        best = (start, u, task)
                    if start <= unit_free[u]:
                        break
        assert best is not None, "task graph has a cycle"
        start, u, task = best
        task.emit()
        task.finish = start + task.cost
        unit_free[u] = task.finish
        pending[u].remove(task)


def _layer_kernel(x_ref, k_ref, v_ref, ng_ref, win_ref, lag_ref, lab_ref, ws_ref, bs_ref,
                  cw_ref, cb_ref, lbg_ref, lbb_ref, wb_ref, wo_ref, fg_ref,
                  o_ref,
                  h_s, v_s, bra_s, brb_s, q_s,
                  z_au, z_av, z_ag, z_bg, z_cg, z_m0, z_m1, z_m2,
                  y_s, sv_s, acc_s, cbuf, *, tm, final):
    scale = tm / 256.0
    mxu, vpu = [], []
    row_blocks = range(0, tm, ROWS)

    def rows(ref, r):
        return ref[r:r + ROWS, :]

    def add(stream, cost, emit, deps=()):
        task = _Task(cost, emit, deps)
        stream.append(task)
        return task

    def chunk_cols(c):
        return slice(c * MXU_COLS, (c + 1) * MXU_COLS)

    def matmul_chunks(lhs_ref, rhs, deps, finish):
        def emit(c):
            finish(c, _dot(lhs_ref[...], rhs(c)))
        return [add(mxu, 512 * scale, functools.partial(emit, c), deps) for c in range(CHUNKS_PER_SLAB)]

    def row_phase(cost, body, deps):
        return [add(vpu, cost, functools.partial(body, r), deps) for r in row_blocks]

    @pl.when(pl.program_id(1) == 0)
    def _():
        cbuf[0:HALO, :] = jnp.zeros((HALO, WIDTH), _F32)

    def rms_body(r):
        h_s[r:r + ROWS, :] = _rms_norm_rows(rows(x_ref, r), ng_ref[...]).astype(_BF16)
    t_rms = row_phase(30, rms_body, ())

    def w_in_chunk(j):
        return lambda c: win_ref[j * CHUNKS_PER_SLAB + c]

    def slab(dst, j, act):
        def finish(c, z):
            dst[:, chunk_cols(c)] = act(z).astype(dst.dtype)
        return matmul_chunks(h_s, w_in_chunk(j), t_rms, finish)

    def glu_emit(c):
        a = _dot(h_s[...], win_ref[SLAB_B_A * CHUNKS_PER_SLAB + c])
        b = _dot(h_s[...], win_ref[SLAB_B_B * CHUNKS_PER_SLAB + c])
        cbuf[HALO:HALO + tm, chunk_cols(c)] = a * _sigmoid(b)
    t_glu = [add(mxu, 1024 * scale, functools.partial(glu_emit, c), t_rms) for c in range(CHUNKS_PER_SLAB)]

    t_av = slab(z_av, SLAB_A_V, _gelu_tanh)

    def v_body(r):
        v_s[r:r + ROWS, :] = _layer_norm_rows(rows(z_av, r), lag_ref[...], lab_ref[...]).astype(_BF16)
    t_v = row_phase(50, v_body, t_av)

    def spatial_body(g):
        tri = (lax.broadcasted_iota(jnp.int32, (CHUNK, CHUNK), 0)
               >= lax.broadcasted_iota(jnp.int32, (CHUNK, CHUNK), 1))
        w_g = jnp.where(tri, ws_ref[g], jnp.zeros((), _BF16))
        cols = slice(g * GROUP_DIM, (g + 1) * GROUP_DIM)
        for c in range(0, tm, CHUNK):
            sv_s[c:c + CHUNK, cols] = _dot(w_g, v_s[c:c + CHUNK, cols])
    t_sp = [add(mxu, 50 * (tm // CHUNK), functools.partial(spatial_body, g), t_v) for g in range(GROUPS)]

    t_au, t_ag = slab(z_au, SLAB_A_U, _gelu_tanh), slab(z_ag, SLAB_A_G, _silu)

    def bra_body(r):
        rc = r % CHUNK
        sv = rows(sv_s, r) + bs_ref[rc:rc + ROWS, :]
        bra_s[r:r + ROWS, :] = (rows(z_au, r) * sv * rows(z_ag, r)).astype(_BF16)
    t_bra = row_phase(30, bra_body, t_sp + t_au + t_ag)

    first = HALO - (CONV_WIDTH - 1)

    def conv_body(r, j, pin=None):
        y = None
        for b in range(SUBLANES):
            n = CONV_ROWS + (SUBLANES if b else 0)
            u = None
            for k in range(CONV_WIDTH):
                off = first + k
                if off % SUBLANES != b:
                    continue
                a = off - b
                w_k = cw_ref[k:k + 1, j:j + LANES]
                if pin is not None and u is None and b == 0:
                    w_k = jnp.where(pl.program_id(0) < 0, pin()[0:1, :], w_k)
                term = cbuf[r + a:r + a + n, j:j + LANES] * w_k
                u = term if u is None else u + term
            if b:
                u = pltpu.roll(u, n - b, axis=0)[0:CONV_ROWS, :]
            y = u if y is None else y + u
        y_s[r:r + CONV_ROWS, j:j + LANES] = y

    t_m0 = slab(z_m0, SLAB_MERGE + 0, _sigmoid)
    t_m1 = slab(z_m1, SLAB_MERGE + 1, _sigmoid)
    t_m2 = slab(z_m2, SLAB_MERGE + 2, _sigmoid)
    t_cg = slab(z_cg, SLAB_C_G, _silu)
    pinned = [(t, functools.partial(lambda buf, c: buf[0:SUBLANES, c * MXU_COLS:c * MXU_COLS + LANES], buf, c))
              for buf, tasks in ((z_m0, t_m0), (z_m1, t_m1), (z_m2, t_m2), (z_cg, t_cg))
              for c, t in enumerate(tasks)]
    blocks = [(r, j) for r in range(0, tm, CONV_ROWS) for j in range(0, WIDTH, LANES)]
    first_pinned, pin_every = 8, 3
    t_conv = []
    for i, (r, j) in enumerate(blocks):
        deps, pin = [t_glu[j // MXU_COLS]], None
        if i >= first_pinned and (i - first_pinned) % pin_every == 0 and (i - first_pinned) // pin_every < len(pinned):
            task, pin = pinned[(i - first_pinned) // pin_every]
            deps.append(task)
        t_conv.append(add(vpu, 170, functools.partial(conv_body, r, j, pin), deps))

    def carry_body():
        cbuf[0:HALO, :] = cbuf[tm:tm + HALO, :]
    add(vpu, 10, carry_body, t_conv)

    t_bg = slab(z_bg, SLAB_B_G, _silu)

    def brb_body(r):
        c = _silu(_layer_norm_rows(rows(y_s, r) + cb_ref[...], lbg_ref[...], lbb_ref[...]))
        brb_s[r:r + ROWS, :] = (c * rows(z_bg, r)).astype(_BF16)
    t_brb = row_phase(80, brb_body, t_conv + t_bg)

    def w_branch_chunk(n):
        return lambda c: wb_ref[n * CHUNKS_PER_SLAB + c]

    def merge_a(c, p):
        acc_s[:, chunk_cols(c)] = z_m0[:, chunk_cols(c)] * p
    t_pa = matmul_chunks(bra_s, w_branch_chunk(0), t_bra + t_m0, merge_a)


    def q_finish(c, z):
        q_s[:, chunk_cols(c)] = z.astype(_BF16)
    t_cq = matmul_chunks(h_s, w_in_chunk(SLAB_C_Q), t_rms, q_finish)
    nt_dims = (((1,), (1,)), ((), ()))

    def scores_body(hd):
        cols = slice(hd * HEAD_DIM, (hd + 1) * HEAD_DIM)
        sv_s[:, hd * MEM_LEN:(hd + 1) * MEM_LEN] = lax.dot_general(
            q_s[:, cols], k_ref[:, cols], nt_dims, preferred_element_type=_F32)
    t_sc = [add(mxu, 100 * scale, functools.partial(scores_body, hd), t_cq + t_bra) for hd in range(HEADS)]

    def softmax_body(r):
        for hd in range(HEADS):
            cols = slice(hd * MEM_LEN, (hd + 1) * MEM_LEN)
            s = sv_s[r:r + ROWS, cols]
            p = jnp.exp(s - jnp.max(s, axis=-1, keepdims=True))
            v_s[r:r + ROWS, cols] = (p * (1.0 / jnp.sum(p, axis=-1, keepdims=True))).astype(_BF16)
    t_sm = row_phase(40, softmax_body, t_sc + t_sp)

    def pv_body(hd):
        cols = slice(hd * HEAD_DIM, (hd + 1) * HEAD_DIM)
        att = _dot(v_s[:, hd * MEM_LEN:(hd + 1) * MEM_LEN], v_ref[:, cols])
        bra_s[:, cols] = (att * z_cg[:, cols]).astype(_BF16)
    t_pv = [add(mxu, 100 * scale, functools.partial(pv_body, hd), t_sm + t_cg + t_pa) for hd in range(HEADS)]

    def merge_c(c, p):
        acc_s[:, chunk_cols(c)] = acc_s[:, chunk_cols(c)] + z_m2[:, chunk_cols(c)] * p
    t_pc = matmul_chunks(bra_s, w_branch_chunk(2), t_pv + t_m2 + t_pa, merge_c)

    def merge_b(c, p):
        merged = acc_s[:, chunk_cols(c)] + z_m1[:, chunk_cols(c)] * p
        bra_s[:, chunk_cols(c)] = merged.astype(_BF16)
    t_pb = matmul_chunks(brb_s, w_branch_chunk(1), t_brb + t_m1 + t_pc, merge_b)

    if final:
        def out_finish(c, p):
            y_s[:, chunk_cols(c)] = x_ref[:, chunk_cols(c)] + p
        t_po = matmul_chunks(bra_s, lambda c: wo_ref[c], t_pb, out_finish)

        def out_body(r):
            o_ref[r:r + ROWS, :] = _rms_norm_rows(rows(y_s, r), fg_ref[...])
        row_phase(30, out_body, t_po)
    else:
        def out_finish(c, p):
            o_ref[:, chunk_cols(c)] = x_ref[:, chunk_cols(c)] + p
        t_po = matmul_chunks(bra_s, lambda c: wo_ref[c], t_pb, out_finish)

    mxu_order = (t_sp + t_pa + t_sc + t_pv + t_pc + t_pb + t_po
                 + t_glu + t_av + t_au + t_ag + t_cq + t_bg + t_m0 + t_cg + t_m1 + t_m2)
    assert len(mxu_order) == len(mxu)
    _trace_two_streams((mxu_order, vpu))


def _layer(x, k_all, v_all, layer, final, norm_g, w_in, lag, lab, w_s, bs_full, conv_w, conv_b,
           lbg, lbb, w_branch, w_out, final_g):
    batch, seq, _ = x.shape
    tm = TILE_TOKENS
    assert seq % tm == 0 and tm % CHUNK == 0 and tm % CONV_ROWS == 0

    def resident(shape):
        nd = len(shape)
        return pl.BlockSpec((None,) + shape, lambda b, s: (layer,) + (0,) * nd,
                            pipeline_mode=pl.Buffered(1))

    tile = pl.BlockSpec((None, tm, D_MODEL), lambda b, s: (b, s, 0))
    kv_spec = pl.BlockSpec((None, None, MEM_LEN, WIDTH), lambda b, s: (layer, b, 0, 0))
    row = (1, WIDTH)
    f32_tile = pltpu.VMEM((tm, WIDTH), _F32)
    bf16_tile = pltpu.VMEM((tm, WIDTH), _BF16)
    return pl.pallas_call(
        functools.partial(_layer_kernel, tm=tm, final=final),
        grid=(batch, seq // tm),
        in_specs=[
            tile, kv_spec, kv_spec,
            resident(row),
            resident((N_SLABS * CHUNKS_PER_SLAB, D_MODEL, MXU_COLS)),
            resident(row), resident(row),
            resident((GROUPS, CHUNK, CHUNK)),
            resident((CHUNK, WIDTH)),
            resident((CONV_WIDTH, WIDTH)),
            resident(row), resident(row), resident(row),
            resident((3 * CHUNKS_PER_SLAB, WIDTH, MXU_COLS)),
            resident((CHUNKS_PER_SLAB, D_MODEL, MXU_COLS)),
            pl.BlockSpec((1, D_MODEL), lambda b, s: (0, 0), pipeline_mode=pl.Buffered(1)),
        ],
        out_specs=tile,
        out_shape=jax.ShapeDtypeStruct(x.shape, x.dtype),
        scratch_shapes=[bf16_tile] * 5 + [f32_tile] * 11 + [pltpu.VMEM((tm + HALO, WIDTH), _F32)],
        compiler_params=pltpu.CompilerParams(
            dimension_semantics=("arbitrary", "arbitrary"),
            vmem_limit_bytes=VMEM_LIMIT_BYTES),
        name=f"trunk_layer_{layer}",
    )(x, k_all, v_all, norm_g, w_in, lag, lab, w_s, bs_full, conv_w, conv_b, lbg, lbb,
      w_branch, w_out, final_g)


def _column_chunks(w):
    *lead, k, n = w.shape
    w = w.reshape(*lead, k, n // MXU_COLS, MXU_COLS)
    return jnp.swapaxes(w, -3, -2)


def kernel(x, mem, norm_g, mem_norm_g, w_in, gmlp_ln_g, gmlp_ln_b, w_s, b_s, conv_w, conv_b,
           conv_ln_g, conv_ln_b, w_kv, w_branch, w_out, final_norm_g):
    depth = w_in.shape[0]
    vec = lambda a: a.reshape(depth, 1, a.shape[-1])
    bs_full = jnp.repeat(jnp.swapaxes(b_s, 1, 2), GROUP_DIM, axis=2)
    k_all, v_all = _keys_values(mem, mem_norm_g, w_kv.astype(_BF16))
    w_s_b = w_s.astype(_BF16)
    w_in_b = _column_chunks(w_in.astype(_BF16))
    w_branch_b = _column_chunks(w_branch.astype(_BF16)).reshape(depth, -1, WIDTH, MXU_COLS)
    w_out_b = _column_chunks(w_out.astype(_BF16))
    final_g = final_norm_g.reshape(1, D_MODEL)
    for layer in range(depth):
        x = _layer(x, k_all, v_all, layer, layer == depth - 1, vec(norm_g), w_in_b,
                   vec(gmlp_ln_g), vec(gmlp_ln_b), w_s_b, bs_full, conv_w, vec(conv_b),
                   vec(conv_ln_g), vec(conv_ln_b), w_branch_b, w_out_b, final_g)
    return x
```

```python
import functools

import jax
import jax.numpy as jnp
from jax import lax
from jax.experimental import pallas as pl
from jax.experimental.pallas import tpu as pltpu

D_MODEL = 1024
WIDTH = 1024
CHUNK = 128
GROUPS = 8
GROUP_DIM = WIDTH // GROUPS
CONV_WIDTH = 31
HEADS = 4
HEAD_DIM = WIDTH // HEADS
MEM_LEN = 256
N_SLABS = 11
N_IN = N_SLABS * WIDTH
RMS_EPS = 1e-6
LN_EPS = 1e-5

SLAB_A_U, SLAB_A_V, SLAB_A_G, SLAB_B_A, SLAB_B_B, SLAB_B_G, SLAB_C_Q, SLAB_C_G = range(8)
SLAB_MERGE = 8

LANES = 128
SUBLANES = 8
ROWS = 16
MXU_COLS = 512
CHUNKS_PER_SLAB = WIDTH // MXU_COLS
PREP_COLS = 1024
HALO = 32
CONV_ROWS = 64
TILE_TOKENS = 256
VMEM_LIMIT_BYTES = 58 * 1024 * 1024

_BF16 = jnp.bfloat16
_F32 = jnp.float32

_GELU_C0 = 0.7978845608028654
_GELU_C1 = _GELU_C0 * 0.044715


def _gelu_tanh(x):
    hx = 0.5 * x
    return hx + hx * jnp.tanh(x * (_GELU_C0 + _GELU_C1 * (x * x)))


def _sigmoid(x):
    return 0.5 + 0.5 * jnp.tanh(0.5 * x)


def _silu(x):
    hx = 0.5 * x
    return hx + hx * jnp.tanh(hx)


def _layer_norm_rows(x, g, b):
    mu = jnp.mean(x, axis=-1, keepdims=True)
    xc = x - mu
    var = jnp.mean(xc * xc, axis=-1, keepdims=True)
    return xc * lax.rsqrt(var + LN_EPS) * g + b


def _rms_norm_rows(x, g):
    return x * lax.rsqrt(jnp.mean(x * x, axis=-1, keepdims=True) + RMS_EPS) * g


def _dot(a, b):
    return jnp.dot(a, b, preferred_element_type=_F32)


def _kv_kernel(mem_ref, g_ref, wkv_ref, k_ref, v_ref):
    mem_n = _rms_norm_rows(mem_ref[...], g_ref[...])
    kv = _dot(mem_n.astype(_BF16), wkv_ref[...].astype(_BF16))
    k_ref[...] = (kv[:, :WIDTH] * (HEAD_DIM ** -0.5)).astype(_BF16)
    v_ref[...] = kv[:, WIDTH:].astype(_BF16)


def _keys_values(mem, mem_norm_g, w_kv):
    depth = w_kv.shape[0]
    batch = mem.shape[0]
    out = jax.ShapeDtypeStruct((depth, batch, MEM_LEN, WIDTH), _BF16)
    kv_spec = pl.BlockSpec((None, None, MEM_LEN, WIDTH), lambda l, b: (l, b, 0, 0))
    return pl.pallas_call(
        _kv_kernel,
        grid=(depth, batch),
        in_specs=[
            pl.BlockSpec((None, MEM_LEN, D_MODEL), lambda l, b: (b, 0, 0)),
            pl.BlockSpec((None, 1, D_MODEL), lambda l, b: (l, 0, 0)),
            pl.BlockSpec((None, D_MODEL, 2 * WIDTH), lambda l, b: (l, 0, 0)),
        ],
        out_specs=[kv_spec, kv_spec],
        out_shape=[out, out],
        compiler_params=pltpu.CompilerParams(dimension_semantics=("arbitrary", "arbitrary"),
                                             vmem_limit_bytes=VMEM_LIMIT_BYTES),
        name="keys_values",
    )(mem, mem_norm_g.reshape(depth, 1, D_MODEL), w_kv)


class _Task:
    def __init__(self, cost, emit, deps=()):
        self.cost, self.emit, self.deps, self.finish = cost, emit, tuple(deps), None


def _trace_two_streams(streams):
    pending = [list(s) for s in streams]
    unit_free = [0.0, 0.0]
    while any(pending):
        best = None
        for u, tasks in enumerate(pending):
            for task in tasks:
                if all(d.finish is not None for d in task.deps):
                    start = max([unit_free[u]] + [d.finish for d in task.deps])
                    if best is None or start < best[0]:
                        best = (start, u, task)
                    if start <= unit_free[u]:
                        break
        assert best is not None, "task graph has a cycle"
        start, u, task = best
        task.emit()
        task.finish = start + task.cost
        unit_free[u] = task.finish
        pending[u].remove(task)


def _layer_kernel(x_ref, xn_ref, k_ref, v_ref, ng_ref, win_ref, lag_ref, lab_ref, ws_ref, bs_ref,
                  cw_ref, cb_ref, lbg_ref, lbb_ref, wb_ref, wo_ref, fg_ref,
                  o_ref,
                  h_s, v_s, bra_s, brb_s, q_s,
                  z_au, z_av, z_ag, z_bg, z_cg, z_m0, z_m1, z_m2,
                  y_s, sv_s, acc_s, cbuf, tok_s, *, tm, final):
    scale = tm / 256.0
    mxu, vpu = [], []
    row_blocks = range(0, tm, ROWS)

    def rows(ref, r):
        return ref[r:r + ROWS, :]

    def add(stream, cost, emit, deps=()):
        task = _Task(cost, emit, deps)
        stream.append(task)
        return task

    def chunk_cols(c):
        return slice(c * MXU_COLS, (c + 1) * MXU_COLS)

    def matmul_chunks(lhs_ref, rhs, deps, finish):
        def emit(c):
            finish(c, _dot(lhs_ref[...], rhs(c)))
        return [add(mxu, 512 * scale, functools.partial(emit, c), deps) for c in range(CHUNKS_PER_SLAB)]

    def row_phase(cost, body, deps):
        return [add(vpu, cost, functools.partial(body, r), deps) for r in row_blocks]

    @pl.when(pl.program_id(1) == 0)
    def _():
        cbuf[0:HALO, :] = jnp.zeros((HALO, WIDTH), _F32)

    def rms_rows(src_ref, r):
        hn = _rms_norm_rows(rows(src_ref, r), ng_ref[...])
        h_s[r:r + ROWS, :] = hn.astype(_BF16)
        return hn

    @pl.when((pl.program_id(0) == 0) & (pl.program_id(1) == 0))
    def _():
        for r in row_blocks:
            rms_rows(x_ref, r)
    t_rms = []

    def w_in_chunk(j):
        return lambda c: win_ref[j * CHUNKS_PER_SLAB + c]

    def slab(dst, j, act):
        def finish(c, z):
            dst[:, chunk_cols(c)] = act(z).astype(dst.dtype)
        return matmul_chunks(h_s, w_in_chunk(j), t_rms, finish)

    def glu_emit(c):
        a = _dot(h_s[...], win_ref[SLAB_B_A * CHUNKS_PER_SLAB + c])
        b = _dot(h_s[...], win_ref[SLAB_B_B * CHUNKS_PER_SLAB + c])
        cbuf[HALO:HALO + tm, chunk_cols(c)] = a * _sigmoid(b)
    t_glu = [add(mxu, 1024 * scale, functools.partial(glu_emit, c), t_rms) for c in range(CHUNKS_PER_SLAB)]

    t_av = slab(z_av, SLAB_A_V, _gelu_tanh)

    def v_body(r):
        v_s[r:r + ROWS, :] = _layer_norm_rows(rows(z_av, r), lag_ref[...], lab_ref[...]).astype(_BF16)
    t_v = row_phase(50, v_body, t_av)

    def spatial_body(g):
        tri = (lax.broadcasted_iota(jnp.int32, (CHUNK, CHUNK), 0)
               >= lax.broadcasted_iota(jnp.int32, (CHUNK, CHUNK), 1))
        w_g = jnp.where(tri, ws_ref[g], jnp.zeros((), _BF16))
        cols = slice(g * GROUP_DIM, (g + 1) * GROUP_DIM)
        for c in range(0, tm, 2 * CHUNK):
            pair = jnp.concatenate([v_s[c:c + CHUNK, cols], v_s[c + CHUNK:c + 2 * CHUNK, cols]], axis=1)
            mixed = _dot(w_g, pair)
            sv_s[c:c + CHUNK, cols] = mixed[:, :GROUP_DIM]
            sv_s[c + CHUNK:c + 2 * CHUNK, cols] = mixed[:, GROUP_DIM:]
    t_sp = [add(mxu, 50 * (tm // CHUNK), functools.partial(spatial_body, g), t_v) for g in range(GROUPS)]

    t_au, t_ag = slab(z_au, SLAB_A_U, _gelu_tanh), slab(z_ag, SLAB_A_G, _silu)

    def bra_body(r):
        rc = r % CHUNK
        sv = rows(sv_s, r) + bs_ref[rc:rc + ROWS, :]
        bra_s[r:r + ROWS, :] = (rows(z_au, r) * sv * rows(z_ag, r)).astype(_BF16)
    t_bra = row_phase(30, bra_body, t_sp + t_au + t_ag)

    first = HALO - (CONV_WIDTH - 1)

    def conv_body(r, j, pin=None):
        y = None
        for b in range(SUBLANES):
            n = CONV_ROWS + (SUBLANES if b else 0)
            u = None
            for k in range(CONV_WIDTH):
                off = first + k
                if off % SUBLANES != b:
                    continue
                a = off - b
                w_k = cw_ref[k:k + 1, j:j + LANES]
                if pin is not None and u is None and b == 0:
                    w_k = jnp.where(pl.program_id(0) < 0, pin()[0:1, :], w_k)
                term = cbuf[r + a:r + a + n, j:j + LANES] * w_k
                u = term if u is None else u + term
            if b:
                u = pltpu.roll(u, n - b, axis=0)[0:CONV_ROWS, :]
            y = u if y is None else y + u
        y_s[r:r + CONV_ROWS, j:j + LANES] = y

    t_m0 = slab(z_m0, SLAB_MERGE + 0, _sigmoid)
    t_m1 = slab(z_m1, SLAB_MERGE + 1, lambda z: z)
    t_m2 = slab(z_m2, SLAB_MERGE + 2, _sigmoid)
    t_cg = slab(z_cg, SLAB_C_G, _silu)
    pinned = [(t, functools.partial(lambda buf, c: buf[0:SUBLANES, c * MXU_COLS:c * MXU_COLS + LANES], buf, c))
              for buf, tasks in ((z_m0, t_m0), (z_m1, t_m1), (z_m2, t_m2), (z_cg, t_cg))
              for c, t in enumerate(tasks)]
    blocks = [(r, j) for r in range(0, tm, CONV_ROWS) for j in range(0, WIDTH, LANES)]
    first_pinned, pin_every = 8, 3
    t_conv = []
    for i, (r, j) in enumerate(blocks):
        deps, pin = [t_glu[j // MXU_COLS]], None
        if i >= first_pinned and (i - first_pinned) % pin_every == 0 and (i - first_pinned) // pin_every < len(pinned):
            task, pin = pinned[(i - first_pinned) // pin_every]
            deps.append(task)
        t_conv.append(add(vpu, 170, functools.partial(conv_body, r, j, pin), deps))

    def carry_body():
        cbuf[0:HALO, :] = cbuf[tm:tm + HALO, :]
    add(vpu, 10, carry_body, t_conv)

    t_bg = slab(z_bg, SLAB_B_G, _silu)

    def brb_body(r):
        c = _silu(_layer_norm_rows(rows(y_s, r) + cb_ref[...], lbg_ref[...], lbb_ref[...]))
        brb_s[r:r + ROWS, :] = (c * rows(z_bg, r)).astype(_BF16)
    t_brb = row_phase(80, brb_body, t_conv + t_bg)

    def w_branch_chunk(n):
        return lambda c: wb_ref[n * CHUNKS_PER_SLAB + c]

    def merge_a(c, p):
        acc_s[:, chunk_cols(c)] = z_m0[:, chunk_cols(c)] * p
    t_pa = matmul_chunks(bra_s, w_branch_chunk(0), t_bra + t_m0, merge_a)


    def q_finish(c, z):
        q_s[:, chunk_cols(c)] = z.astype(_BF16)
    t_cq = matmul_chunks(h_s, w_in_chunk(SLAB_C_Q), t_rms, q_finish)
    nt_dims = (((1,), (1,)), ((), ()))

    def scores_body(hd):
        cols = slice(hd * HEAD_DIM, (hd + 1) * HEAD_DIM)
        sv_s[:, hd * MEM_LEN:(hd + 1) * MEM_LEN] = lax.dot_general(
            q_s[:, cols], k_ref[:, cols], nt_dims, preferred_element_type=_F32)
    t_sc = [add(mxu, 100 * scale, functools.partial(scores_body, hd), t_cq + t_bra) for hd in range(HEADS)]

    def softmax_body(r):
        for hd in range(HEADS):
            cols = slice(hd * MEM_LEN, (hd + 1) * MEM_LEN)
            s = sv_s[r:r + ROWS, cols]
            p = jnp.exp(s - jnp.max(s, axis=-1, keepdims=True))
            v_s[r:r + ROWS, cols] = (p * (1.0 / jnp.sum(p, axis=-1, keepdims=True))).astype(_BF16)
    t_sm = row_phase(40, softmax_body, t_sc + t_sp)

    def pv_body(hd):
        cols = slice(hd * HEAD_DIM, (hd + 1) * HEAD_DIM)
        att = _dot(v_s[:, hd * MEM_LEN:(hd + 1) * MEM_LEN], v_ref[:, cols])
        bra_s[:, cols] = (att * z_cg[:, cols]).astype(_BF16)
    t_pv = [add(mxu, 100 * scale, functools.partial(pv_body, hd), t_sm + t_cg + t_pa) for hd in range(HEADS)]

    def merge_c(c, p):
        acc_s[:, chunk_cols(c)] = acc_s[:, chunk_cols(c)] + z_m2[:, chunk_cols(c)] * p
    t_pc = matmul_chunks(bra_s, w_branch_chunk(2), t_pv + t_m2 + t_pa, merge_c)

    def merge_b(c, p):
        merged = acc_s[:, chunk_cols(c)] + _sigmoid(z_m1[:, chunk_cols(c)]) * p
        bra_s[:, chunk_cols(c)] = merged.astype(_BF16)
        if c == CHUNKS_PER_SLAB - 1:
            c0 = c * MXU_COLS
            pinned = jnp.where(pl.program_id(0) < 0, tok_s[...], merged[0:tm // 2, 0:LANES])
            bra_s[0:tm // 2, c0:c0 + LANES] = pinned.astype(_BF16)

    def rms_next_body(r):
        tok_s[r // 2:r // 2 + SUBLANES, :] = rms_rows(xn_ref, r)[0:SUBLANES, 0:LANES]
    t_next = row_phase(30, rms_next_body,
                       t_glu + t_av + t_au + t_ag + t_bg + t_cq + t_cg + t_m0 + t_m1 + t_m2)
    t_pb = matmul_chunks(brb_s, w_branch_chunk(1), t_brb + t_m1 + t_pc + t_next, merge_b)

    if final:
        def out_finish(c, p):
            y_s[:, chunk_cols(c)] = x_ref[:, chunk_cols(c)] + p
        t_po = matmul_chunks(bra_s, lambda c: wo_ref[c], t_pb, out_finish)

        def out_body(r):
            o_ref[r:r + ROWS, :] = _rms_norm_rows(rows(y_s, r), fg_ref[...])
        row_phase(30, out_body, t_po)
    else:
        def out_finish(c, p):
            o_ref[:, chunk_cols(c)] = x_ref[:, chunk_cols(c)] + p
        t_po = matmul_chunks(bra_s, lambda c: wo_ref[c], t_pb, out_finish)

    mxu_order = (t_sp + t_pa + t_sc + t_pv + t_pc + t_pb + t_po
                 + t_glu + t_av + t_au + t_ag + t_cq + t_bg + t_m0 + t_cg + t_m1 + t_m2)
    assert len(mxu_order) == len(mxu)
    _trace_two_streams((mxu_order, vpu))


def _layer(x, k_all, v_all, layer, final, norm_g, w_in, lag, lab, w_s, bs_full, conv_w, conv_b,
           lbg, lbb, w_branch, w_out, final_g):
    batch, seq, _ = x.shape
    tm = TILE_TOKENS
    assert seq % tm == 0 and tm % CHUNK == 0 and tm % CONV_ROWS == 0

    def resident(shape):
        nd = len(shape)
        return pl.BlockSpec((None,) + shape, lambda b, s: (layer,) + (0,) * nd,
                            pipeline_mode=pl.Buffered(1))

    tile = pl.BlockSpec((None, tm, D_MODEL), lambda b, s: (b, s, 0))
    n_s = seq // tm

    def next_tile(b, s):
        t = jnp.minimum(b * n_s + s + 1, batch * n_s - 1)
        return (t // n_s, t % n_s, 0)
    kv_spec = pl.BlockSpec((None, None, MEM_LEN, WIDTH), lambda b, s: (layer, b, 0, 0))
    row = (1, WIDTH)
    f32_tile = pltpu.VMEM((tm, WIDTH), _F32)
    bf16_tile = pltpu.VMEM((tm, WIDTH), _BF16)
    return pl.pallas_call(
        functools.partial(_layer_kernel, tm=tm, final=final),
        grid=(batch, seq // tm),
        in_specs=[
            tile, pl.BlockSpec((None, tm, D_MODEL), next_tile), kv_spec, kv_spec,
            resident(row),
            resident((N_SLABS * CHUNKS_PER_SLAB, D_MODEL, MXU_COLS)),
            resident(row), resident(row),
            resident((GROUPS, CHUNK, CHUNK)),
            resident((CHUNK, WIDTH)),
            resident((CONV_WIDTH, WIDTH)),
            resident(row), resident(row), resident(row),
            resident((3 * CHUNKS_PER_SLAB, WIDTH, MXU_COLS)),
            resident((CHUNKS_PER_SLAB, D_MODEL, MXU_COLS)),
            pl.BlockSpec((1, D_MODEL), lambda b, s: (0, 0), pipeline_mode=pl.Buffered(1)),
        ],
        out_specs=tile,
        out_shape=jax.ShapeDtypeStruct(x.shape, x.dtype),
        scratch_shapes=[bf16_tile] * 5 + [f32_tile] * 11 + [pltpu.VMEM((tm + HALO, WIDTH), _F32),
                                                              pltpu.VMEM((tm // 2, LANES), _F32)],
        compiler_params=pltpu.CompilerParams(
            dimension_semantics=("arbitrary", "arbitrary"),
            vmem_limit_bytes=VMEM_LIMIT_BYTES),
        name=f"trunk_layer_{layer}",
    )(x, x, k_all, v_all, norm_g, w_in, lag, lab, w_s, bs_full, conv_w, conv_b, lbg, lbb,
      w_branch, w_out, final_g)


def _chunk_kernel(w_ref, o_ref):
    for i in range(PREP_COLS // MXU_COLS):
        o_ref[i] = w_ref[:, i * MXU_COLS:(i + 1) * MXU_COLS].astype(_BF16)


def _column_chunks(w):
    n_mats, k, n = w.shape
    per_block = PREP_COLS // MXU_COLS
    return pl.pallas_call(
        _chunk_kernel,
        grid=(n_mats, n // PREP_COLS),
        in_specs=[pl.BlockSpec((None, k, PREP_COLS), lambda l, j: (l, 0, j))],
        out_specs=pl.BlockSpec((None, per_block, k, MXU_COLS), lambda l, j: (l, j, 0, 0)),
        out_shape=jax.ShapeDtypeStruct((n_mats, n // MXU_COLS, k, MXU_COLS), _BF16),
        compiler_params=pltpu.CompilerParams(dimension_semantics=("arbitrary", "arbitrary")),
        name="weight_chunks",
    )(w)


def kernel(x, mem, norm_g, mem_norm_g, w_in, gmlp_ln_g, gmlp_ln_b, w_s, b_s, conv_w, conv_b,
           conv_ln_g, conv_ln_b, w_kv, w_branch, w_out, final_norm_g):
    depth = w_in.shape[0]
    vec = lambda a: a.reshape(depth, 1, a.shape[-1])
    bs_full = jnp.repeat(jnp.swapaxes(b_s, 1, 2), GROUP_DIM, axis=2)
    k_all, v_all = _keys_values(mem, mem_norm_g, w_kv)
    w_s_b = w_s.astype(_BF16)
    w_in_b = _column_chunks(w_in)
    w_branch_b = _column_chunks(w_branch.reshape(-1, WIDTH, D_MODEL)).reshape(depth, -1, WIDTH, MXU_COLS)
    w_out_b = _column_chunks(w_out)
    final_g = final_norm_g.reshape(1, D_MODEL)
    for layer in range(depth):
        x = _layer(x, k_all, v_all, layer, layer == depth - 1, vec(norm_g), w_in_b,
                   vec(gmlp_ln_g), vec(gmlp_ln_b), w_s_b, bs_full, conv_w, vec(conv_b),
                   vec(conv_ln_g), vec(conv_ln_b), w_branch_b, w_out_b, final_g)
    return x
```

```python
import functools

import jax
import jax.numpy as jnp
from jax import lax
from jax.experimental import pallas as pl
from jax.experimental.pallas import tpu as pltpu

D_MODEL = 1024
WIDTH = 1024
CHUNK = 128
GROUPS = 8
GROUP_DIM = WIDTH // GROUPS
CONV_WIDTH = 31
HEADS = 4
HEAD_DIM = WIDTH // HEADS
MEM_LEN = 256
N_SLABS = 11
N_IN = N_SLABS * WIDTH
RMS_EPS = 1e-6
LN_EPS = 1e-5

SLAB_A_U, SLAB_A_V, SLAB_A_G, SLAB_B_A, SLAB_B_B, SLAB_B_G, SLAB_C_Q, SLAB_C_G = range(8)
SLAB_MERGE = 8

LANES = 128
SUBLANES = 8
ROWS = 16
MXU_COLS = 512
CHUNKS_PER_SLAB = WIDTH // MXU_COLS
PREP_COLS = 1024
HALO = 32
CONV_ROWS = 64
TILE_TOKENS = 256
VMEM_LIMIT_BYTES = 58 * 1024 * 1024

_BF16 = jnp.bfloat16
_F32 = jnp.float32

_GELU_C0 = 0.7978845608028654
_GELU_C1 = _GELU_C0 * 0.044715


def _gelu_tanh(x):
    hx = 0.5 * x
    return hx + hx * jnp.tanh(x * (_GELU_C0 + _GELU_C1 * (x * x)))


def _sigmoid(x):
    return 0.5 + 0.5 * jnp.tanh(0.5 * x)


def _silu(x):
    hx = 0.5 * x
    return hx + hx * jnp.tanh(hx)


def _layer_norm_rows(x, g, b):
    mu = jnp.mean(x, axis=-1, keepdims=True)
    xc = x - mu
    var = jnp.mean(xc * xc, axis=-1, keepdims=True)
    return xc * lax.rsqrt(var + LN_EPS) * g + b


def _rms_norm_rows(x, g):
    return x * lax.rsqrt(jnp.mean(x * x, axis=-1, keepdims=True) + RMS_EPS) * g


def _dot(a, b):
    return jnp.dot(a, b, preferred_element_type=_F32)


def _kv_kernel(mem_ref, g_ref, wkv_ref, k_ref, v_ref):
    mem_n = _rms_norm_rows(mem_ref[...], g_ref[...])
    kv = _dot(mem_n.astype(_BF16), wkv_ref[...].astype(_BF16))
    k_ref[...] = (kv[:, :WIDTH] * (HEAD_DIM ** -0.5)).astype(_BF16)
    v_ref[...] = kv[:, WIDTH:].astype(_BF16)


def _keys_values(mem, mem_norm_g, w_kv):
    depth = w_kv.shape[0]
    batch = mem.shape[0]
    out = jax.ShapeDtypeStruct((depth, batch, MEM_LEN, WIDTH), _BF16)
    kv_spec = pl.BlockSpec((None, None, MEM_LEN, WIDTH), lambda l, b: (l, b, 0, 0))
    return pl.pallas_call(
        _kv_kernel,
        grid=(depth, batch),
        in_specs=[
            pl.BlockSpec((None, MEM_LEN, D_MODEL), lambda l, b: (b, 0, 0)),
            pl.BlockSpec((None, 1, D_MODEL), lambda l, b: (l, 0, 0)),
            pl.BlockSpec((None, D_MODEL, 2 * WIDTH), lambda l, b: (l, 0, 0)),
        ],
        out_specs=[kv_spec, kv_spec],
        out_shape=[out, out],
        compiler_params=pltpu.CompilerParams(dimension_semantics=("arbitrary", "arbitrary"),
                                             vmem_limit_bytes=VMEM_LIMIT_BYTES),
        name="keys_values",
    )(mem, mem_norm_g.reshape(depth, 1, D_MODEL), w_kv)


class _Task:
    def __init__(self, cost, emit, deps=()):
        self.cost, self.emit, self.deps, self.finish = cost, emit, tuple(deps), None


def _trace_two_streams(streams):
    pending = [list(s) for s in streams]
    unit_free = [0.0, 0.0]
    while any(pending):
        best = None
        for u, tasks in enumerate(pending):
            for task in tasks:
                if all(d.finish is not None for d in task.deps):
                    start = max([unit_free[u]] + [d.finish for d in task.deps])
                    if best is None or start < best[0]:
                        best = (start, u, task)
                    if start <= unit_free[u]:
                        break
        assert best is not None, "task graph has a cycle"
        start, u, task = best
        task.emit()
        task.finish = start + task.cost
        unit_free[u] = task.finish
        pending[u].remove(task)


def _layer_kernel(x_ref, k_ref, v_ref, ng_ref, win_ref, lag_ref, lab_ref, ws_ref, bs_ref,
                  cw_ref, cb_ref, lbg_ref, lbb_ref, wb_ref, wo_ref, fg_ref,
                  o_ref,
                  h_s, v_s, bra_s, brb_s, q_s,
                  z_au, z_av, z_ag, z_bg, z_cg, z_m0, z_m1, z_m2,
                  y_s, sv_s, acc_s, cbuf, *, tm, final):
    scale = tm / 256.0
    mxu, vpu = [], []
    row_blocks = range(0, tm, ROWS)

    def rows(ref, r):
        return ref[r:r + ROWS, :]

    def add(stream, cost, emit, deps=()):
        task = _Task(cost, emit, deps)
        stream.append(task)
        return task

    def chunk_cols(c):
        return slice(c * MXU_COLS, (c + 1) * MXU_COLS)

    def matmul_chunks(lhs_ref, rhs, deps, finish):
        def emit(c):
            finish(c, _dot(lhs_ref[...], rhs(c)))
        return [add(mxu, 512 * scale, functools.partial(emit, c), deps) for c in range(CHUNKS_PER_SLAB)]

    def row_phase(cost, body, deps):
        return [add(vpu, cost, functools.partial(body, r), deps) for r in row_blocks]

    @pl.when(pl.program_id(1) == 0)
    def _():
        cbuf[0:HALO, :] = jnp.zeros((HALO, WIDTH), _F32)

    def rms_body(r):
        h_s[r:r + ROWS, :] = _rms_norm_rows(rows(x_ref, r), ng_ref[...]).astype(_BF16)
    t_rms = row_phase(30, rms_body, ())

    def w_in_chunk(j):
        return lambda c: win_ref[j * CHUNKS_PER_SLAB + c]

    def slab(dst, j, act):
        def finish(c, z):
            dst[:, chunk_cols(c)] = act(z).astype(dst.dtype)
        return matmul_chunks(h_s, w_in_chunk(j), t_rms, finish)

    def glu_emit(c):
        a = _dot(h_s[...], win_ref[SLAB_B_A * CHUNKS_PER_SLAB + c])
        b = _dot(h_s[...], win_ref[SLAB_B_B * CHUNKS_PER_SLAB + c])
        cbuf[HALO:HALO + tm, chunk_cols(c)] = a * _sigmoid(b)
    t_glu = [add(mxu, 1024 * scale, functools.partial(glu_emit, c), t_rms) for c in range(CHUNKS_PER_SLAB)]

    t_av = slab(z_av, SLAB_A_V, _gelu_tanh)

    def v_body(r):
        v_s[r:r + ROWS, :] = _layer_norm_rows(rows(z_av, r), lag_ref[...], lab_ref[...]).astype(_BF16)
    t_v = row_phase(50, v_body, t_av)

    def spatial_body(g):
        tri = (lax.broadcasted_iota(jnp.int32, (CHUNK, CHUNK), 0)
               >= lax.broadcasted_iota(jnp.int32, (CHUNK, CHUNK), 1))
        w_g = jnp.where(tri, ws_ref[g], jnp.zeros((), _BF16))
        cols = slice(g * GROUP_DIM, (g + 1) * GROUP_DIM)
        for c in range(0, tm, CHUNK):
            sv_s[c:c + CHUNK, cols] = _dot(w_g, v_s[c:c + CHUNK, cols])
    t_sp = [add(mxu, 50 * (tm // CHUNK), functools.partial(spatial_body, g), t_v) for g in range(GROUPS)]

    t_au, t_ag = slab(z_au, SLAB_A_U, _gelu_tanh), slab(z_ag, SLAB_A_G, _silu)

    def bra_body(r):
        rc = r % CHUNK
        sv = rows(sv_s, r) + bs_ref[rc:rc + ROWS, :]
        bra_s[r:r + ROWS, :] = (rows(z_au, r) * sv * rows(z_ag, r)).astype(_BF16)
    t_bra = row_phase(30, bra_body, t_sp + t_au + t_ag)

    first = HALO - (CONV_WIDTH - 1)

    def conv_body(r, j, pin=None):
        y = None
        for b in range(SUBLANES):
            n = CONV_ROWS + (SUBLANES if b else 0)
            u = None
            for k in range(CONV_WIDTH):
                off = first + k
                if off % SUBLANES != b:
                    continue
                a = off - b
                w_k = cw_ref[k:k + 1, j:j + LANES]
                if pin is not None and u is None and b == 0:
                    w_k = jnp.where(pl.program_id(0) < 0, pin()[0:1, :], w_k)
                term = cbuf[r + a:r + a + n, j:j + LANES] * w_k
                u = term if u is None else u + term
            if b:
                u = pltpu.roll(u, n - b, axis=0)[0:CONV_ROWS, :]
            y = u if y is None else y + u
        y_s[r:r + CONV_ROWS, j:j + LANES] = y

    t_m0 = slab(z_m0, SLAB_MERGE + 0, _sigmoid)
    t_m1 = slab(z_m1, SLAB_MERGE + 1, _sigmoid)
    t_m2 = slab(z_m2, SLAB_MERGE + 2, _sigmoid)
    t_cg = slab(z_cg, SLAB_C_G, _silu)
    pinned = [(t, functools.partial(lambda buf, c: buf[0:SUBLANES, c * MXU_COLS:c * MXU_COLS + LANES], buf, c))
              for buf, tasks in ((z_m0, t_m0), (z_m1, t_m1), (z_m2, t_m2), (z_cg, t_cg))
              for c, t in enumerate(tasks)]
    blocks = [(r, j) for r in range(0, tm, CONV_ROWS) for j in range(0, WIDTH, LANES)]
    first_pinned, pin_every = 8, 3
    t_conv = []
    for i, (r, j) in enumerate(blocks):
        deps, pin = [t_glu[j // MXU_COLS]], None
        if i >= first_pinned and (i - first_pinned) % pin_every == 0 and (i - first_pinned) // pin_every < len(pinned):
            task, pin = pinned[(i - first_pinned) // pin_every]
            deps.append(task)
        t_conv.append(add(vpu, 170, functools.partial(conv_body, r, j, pin), deps))

    def carry_body():
        cbuf[0:HALO, :] = cbuf[tm:tm + HALO, :]
    add(vpu, 10, carry_body, t_conv)

    t_bg = slab(z_bg, SLAB_B_G, _silu)

    def brb_body(r):
        c = _silu(_layer_norm_rows(rows(y_s, r) + cb_ref[...], lbg_ref[...], lbb_ref[...]))
        brb_s[r:r + ROWS, :] = (c * rows(z_bg, r)).astype(_BF16)
    t_brb = row_phase(80, brb_body, t_conv + t_bg)

    def w_branch_chunk(n):
        return lambda c: wb_ref[n * CHUNKS_PER_SLAB + c]

    def merge_a(c, p):
        acc_s[:, chunk_cols(c)] = z_m0[:, chunk_cols(c)] * p
    t_pa = matmul_chunks(bra_s, w_branch_chunk(0), t_bra + t_m0, merge_a)


    def q_finish(c, z):
        q_s[:, chunk_cols(c)] = z.astype(_BF16)
    t_cq = matmul_chunks(h_s, w_in_chunk(SLAB_C_Q), t_rms, q_finish)
    nt_dims = (((1,), (1,)), ((), ()))

    def scores_body(hd):
        cols = slice(hd * HEAD_DIM, (hd + 1) * HEAD_DIM)
        sv_s[:, hd * MEM_LEN:(hd + 1) * MEM_LEN] = lax.dot_general(
            q_s[:, cols], k_ref[:, cols], nt_dims, preferred_element_type=_F32)
    t_sc = [add(mxu, 100 * scale, functools.partial(scores_body, hd), t_cq + t_bra) for hd in range(HEADS)]

    def softmax_body(r):
        for hd in range(HEADS):
            cols = slice(hd * MEM_LEN, (hd + 1) * MEM_LEN)
            s = sv_s[r:r + ROWS, cols]
            p = jnp.exp(s - jnp.max(s, axis=-1, keepdims=True))
            v_s[r:r + ROWS, cols] = (p * (1.0 / jnp.sum(p, axis=-1, keepdims=True))).astype(_BF16)
    t_sm = row_phase(40, softmax_body, t_sc + t_sp)

    def pv_body(hd):
        cols = slice(hd * HEAD_DIM, (hd + 1) * HEAD_DIM)
        att = _dot(v_s[:, hd * MEM_LEN:(hd + 1) * MEM_LEN], v_ref[:, cols])
        bra_s[:, cols] = (att * z_cg[:, cols]).astype(_BF16)
    t_pv = [add(mxu, 100 * scale, functools.partial(pv_body, hd), t_sm + t_cg + t_pa) for hd in range(HEADS)]

    def merge_c(c, p):
        acc_s[:, chunk_cols(c)] = acc_s[:, chunk_cols(c)] + z_m2[:, chunk_cols(c)] * p
    t_pc = matmul_chunks(bra_s, w_branch_chunk(2), t_pv + t_m2 + t_pa, merge_c)

    def merge_b(c, p):
        merged = acc_s[:, chunk_cols(c)] + z_m1[:, chunk_cols(c)] * p
        bra_s[:, chunk_cols(c)] = merged.astype(_BF16)
    t_pb = matmul_chunks(brb_s, w_branch_chunk(1), t_brb + t_m1 + t_pc, merge_b)

    if final:
        def out_finish(c, p):
            y_s[:, chunk_cols(c)] = x_ref[:, chunk_cols(c)] + p
        t_po = matmul_chunks(bra_s, lambda c: wo_ref[c], t_pb, out_finish)

        def out_body(r):
            o_ref[r:r + ROWS, :] = _rms_norm_rows(rows(y_s, r), fg_ref[...])
        row_phase(30, out_body, t_po)
    else:
        def out_finish(c, p):
            o_ref[:, chunk_cols(c)] = x_ref[:, chunk_cols(c)] + p
        t_po = matmul_chunks(bra_s, lambda c: wo_ref[c], t_pb, out_finish)

    mxu_order = (t_sp + t_pa + t_sc + t_pv + t_pc + t_pb + t_po
                 + t_glu + t_av + t_au + t_ag + t_cq + t_bg + t_m0 + t_cg + t_m1 + t_m2)
    assert len(mxu_order) == len(mxu)
    _trace_two_streams((mxu_order, vpu))


def _layer(x, k_all, v_all, layer, final, norm_g, w_in, lag, lab, w_s, bs_full, conv_w, conv_b,
           lbg, lbb, w_branch, w_out, final_g):
    batch, seq, _ = x.shape
    tm = TILE_TOKENS
    assert seq % tm == 0 and tm % CHUNK == 0 and tm % CONV_ROWS == 0

    def resident(shape):
        nd = len(shape)
        return pl.BlockSpec((None,) + shape, lambda b, s: (layer,) + (0,) * nd,
                            pipeline_mode=pl.Buffered(1))

    tile = pl.BlockSpec((None, tm, D_MODEL), lambda b, s: (b, s, 0))
    kv_spec = pl.BlockSpec((None, None, MEM_LEN, WIDTH), lambda b, s: (layer, b, 0, 0))
    row = (1, WIDTH)
    f32_tile = pltpu.VMEM((tm, WIDTH), _F32)
    bf16_tile = pltpu.VMEM((tm, WIDTH), _BF16)
    return pl.pallas_call(
        functools.partial(_layer_kernel, tm=tm, final=final),
        grid=(batch, seq // tm),
        in_specs=[
            tile, kv_spec, kv_spec,
            resident(row),
            resident((N_SLABS * CHUNKS_PER_SLAB, D_MODEL, MXU_COLS)),
            resident(row), resident(row),
            resident((GROUPS, CHUNK, CHUNK)),
            resident((CHUNK, WIDTH)),
            resident((CONV_WIDTH, WIDTH)),
            resident(row), resident(row), resident(row),
            resident((3 * CHUNKS_PER_SLAB, WIDTH, MXU_COLS)),
            resident((CHUNKS_PER_SLAB, D_MODEL, MXU_COLS)),
            pl.BlockSpec((1, D_MODEL), lambda b, s: (0, 0), pipeline_mode=pl.Buffered(1)),
        ],
        out_specs=tile,
        out_shape=jax.ShapeDtypeStruct(x.shape, x.dtype),
        scratch_shapes=[bf16_tile] * 5 + [f32_tile] * 11 + [pltpu.VMEM((tm + HALO, WIDTH), _F32)],
        compiler_params=pltpu.CompilerParams(
            dimension_semantics=("arbitrary", "arbitrary"),
            vmem_limit_bytes=VMEM_LIMIT_BYTES),
        name=f"trunk_layer_{layer}",
    )(x, k_all, v_all, norm_g, w_in, lag, lab, w_s, bs_full, conv_w, conv_b, lbg, lbb,
      w_branch, w_out, final_g)


def _chunk_kernel(w_ref, o_ref):
    for i in range(PREP_COLS // MXU_COLS):
        o_ref[i] = w_ref[:, i * MXU_COLS:(i + 1) * MXU_COLS].astype(_BF16)


def _column_chunks(w):
    n_mats, k, n = w.shape
    per_block = PREP_COLS // MXU_COLS
    return pl.pallas_call(
        _chunk_kernel,
        grid=(n_mats, n // PREP_COLS),
        in_specs=[pl.BlockSpec((None, k, PREP_COLS), lambda l, j: (l, 0, j))],
        out_specs=pl.BlockSpec((None, per_block, k, MXU_COLS), lambda l, j: (l, j, 0, 0)),
        out_shape=jax.ShapeDtypeStruct((n_mats, n // MXU_COLS, k, MXU_COLS), _BF16),
        compiler_params=pltpu.CompilerParams(dimension_semantics=("arbitrary", "arbitrary")),
        name="weight_chunks",
    )(w)


def kernel(x, mem, norm_g, mem_norm_g, w_in, gmlp_ln_g, gmlp_ln_b, w_s, b_s, conv_w, conv_b,
           conv_ln_g, conv_ln_b, w_kv, w_branch, w_out, final_norm_g):
    depth = w_in.shape[0]
    vec = lambda a: a.reshape(depth, 1, a.shape[-1])
    bs_full = jnp.repeat(jnp.swapaxes(b_s, 1, 2), GROUP_DIM, axis=2)
    k_all, v_all = _keys_values(mem, mem_norm_g, w_kv)
    w_s_b = w_s.astype(_BF16)
    w_in_b = _column_chunks(w_in)
    w_branch_b = _column_chunks(w_branch.reshape(-1, WIDTH, D_MODEL)).reshape(depth, -1, WIDTH, MXU_COLS)
    w_out_b = _column_chunks(w_out)
    final_g = final_norm_g.reshape(1, D_MODEL)
    for layer in range(depth):
        x = _layer(x, k_all, v_all, layer, layer == depth - 1, vec(norm_g), w_in_b,
                   vec(gmlp_ln_g), vec(gmlp_ln_b), w_s_b, bs_full, conv_w, vec(conv_b),
                   vec(conv_ln_g), vec(conv_ln_b), w_branch_b, w_out_b, final_g)
    return x
```

```python
import functools

import jax
import jax.numpy as jnp
from jax import lax
from jax.experimental import pallas as pl
from jax.experimental.pallas import tpu as pltpu

D_MODEL = 1024
WIDTH = 1024
CHUNK = 128
GROUPS = 8
GROUP_DIM = WIDTH // GROUPS
CONV_WIDTH = 31
HEADS = 4
HEAD_DIM = WIDTH // HEADS
MEM_LEN = 256
N_SLABS = 11
N_IN = N_SLABS * WIDTH
RMS_EPS = 1e-6
LN_EPS = 1e-5

SLAB_A_U, SLAB_A_V, SLAB_A_G, SLAB_B_A, SLAB_B_B, SLAB_B_G, SLAB_C_Q, SLAB_C_G = range(8)
SLAB_MERGE = 8

LANES = 128
SUBLANES = 8
ROWS = 16
MXU_COLS = 512
CHUNKS_PER_SLAB = WIDTH // MXU_COLS
PREP_COLS = 1024
HALO = 32
CONV_ROWS = 64
TILE_TOKENS = 256
VMEM_LIMIT_BYTES = 58 * 1024 * 1024

_BF16 = jnp.bfloat16
_F32 = jnp.float32

_GELU_C0 = 0.7978845608028654
_GELU_C1 = _GELU_C0 * 0.044715


def _gelu_tanh(x):
    hx = 0.5 * x
    return hx + hx * jnp.tanh(x * (_GELU_C0 + _GELU_C1 * (x * x)))


def _sigmoid(x):
    return 0.5 + 0.5 * jnp.tanh(0.5 * x)


def _silu(x):
    hx = 0.5 * x
    return hx + hx * jnp.tanh(hx)


def _layer_norm_rows(x, g, b):
    mu = jnp.mean(x, axis=-1, keepdims=True)
    xc = x - mu
    var = jnp.mean(xc * xc, axis=-1, keepdims=True)
    return xc * lax.rsqrt(var + LN_EPS) * g + b


def _rms_norm_rows(x, g):
    return x * lax.rsqrt(jnp.mean(x * x, axis=-1, keepdims=True) + RMS_EPS) * g


def _dot(a, b):
    return jnp.dot(a, b, preferred_element_type=_F32)


def _kv_kernel(mem_ref, g_ref, wkv_ref, k_ref, v_ref):
    mem_n = _rms_norm_rows(mem_ref[...], g_ref[...])
    kv = _dot(mem_n.astype(_BF16), wkv_ref[...].astype(_BF16))
    k_ref[...] = (kv[:, :WIDTH] * (HEAD_DIM ** -0.5)).astype(_BF16)
    v_ref[...] = kv[:, WIDTH:].astype(_BF16)


def _keys_values(mem, mem_norm_g, w_kv):
    depth = w_kv.shape[0]
    batch = mem.shape[0]
    out = jax.ShapeDtypeStruct((depth, batch, MEM_LEN, WIDTH), _BF16)
    kv_spec = pl.BlockSpec((None, None, MEM_LEN, WIDTH), lambda l, b: (l, b, 0, 0))
    return pl.pallas_call(
        _kv_kernel,
        grid=(depth, batch),
        in_specs=[
            pl.BlockSpec((None, MEM_LEN, D_MODEL), lambda l, b: (b, 0, 0)),
            pl.BlockSpec((None, 1, D_MODEL), lambda l, b: (l, 0, 0)),
            pl.BlockSpec((None, D_MODEL, 2 * WIDTH), lambda l, b: (l, 0, 0)),
        ],
        out_specs=[kv_spec, kv_spec],
        out_shape=[out, out],
        compiler_params=pltpu.CompilerParams(dimension_semantics=("arbitrary", "arbitrary"),
                                             vmem_limit_bytes=VMEM_LIMIT_BYTES),
        name="keys_values",
    )(mem, mem_norm_g.reshape(depth, 1, D_MODEL), w_kv)


class _Task:
    def __init__(self, cost, emit, deps=()):
        self.cost, self.emit, self.deps, self.finish = cost, emit, tuple(deps), None


def _trace_two_streams(streams):
    pending = [list(s) for s in streams]
    unit_free = [0.0, 0.0]
    while any(pending):
        best = None
        for u, tasks in enumerate(pending):
            for task in tasks:
                if all(d.finish is not None for d in task.deps):
                    start = max([unit_free[u]] + [d.finish for d in task.deps])
                    if best is None or start < best[0]:
                        best = (start, u, task)
                    if start <= unit_free[u]:
                        break
        assert best is not None, "task graph has a cycle"
        start, u, task = best
        task.emit()
        task.finish = start + task.cost
        unit_free[u] = task.finish
        pending[u].remove(task)


def _layer_kernel(x_ref, k_ref, v_ref, ng_ref, win_ref, lag_ref, lab_ref, ws_ref, bs_ref,
                  cw_ref, cb_ref, lbg_ref, lbb_ref, wb_ref, wo_ref, fg_ref,
                  o_ref,
                  h_s, v_s, bra_s, brb_s, q_s,
                  z_au, z_av, z_ag, z_bg, z_cg, z_m0, z_m1, z_m2,
                  y_s, sv_s, acc_s, cbuf, *, tm, final):
    scale = tm / 256.0
    mxu, vpu = [], []
    row_blocks = range(0, tm, ROWS)

    def rows(ref, r):
        return ref[r:r + ROWS, :]

    def add(stream, cost, emit, deps=()):
        task = _Task(cost, emit, deps)
        stream.append(task)
        return task

    def chunk_cols(c):
        return slice(c * MXU_COLS, (c + 1) * MXU_COLS)

    def matmul_chunks(lhs_ref, rhs, deps, finish):
        def emit(c):
            finish(c, _dot(lhs_ref[...], rhs(c)))
        return [add(mxu, 512 * scale, functools.partial(emit, c), deps) for c in range(CHUNKS_PER_SLAB)]

    def row_phase(cost, body, deps):
        return [add(vpu, cost, functools.partial(body, r), deps) for r in row_blocks]

    @pl.when(pl.program_id(1) == 0)
    def _():
        cbuf[0:HALO, :] = jnp.zeros((HALO, WIDTH), _F32)

    def rms_body(r):
        h_s[r:r + ROWS, :] = _rms_norm_rows(rows(x_ref, r), ng_ref[...]).astype(_BF16)
    t_rms = row_phase(30, rms_body, ())

    def w_in_chunk(j):
        return lambda c: win_ref[j * CHUNKS_PER_SLAB + c]

    def slab(dst, j, act):
        def finish(c, z):
            dst[:, chunk_cols(c)] = act(z).astype(dst.dtype)
        return matmul_chunks(h_s, w_in_chunk(j), t_rms, finish)

    def glu_emit(c):
        a = _dot(h_s[...], win_ref[SLAB_B_A * CHUNKS_PER_SLAB + c])
        b = _dot(h_s[...], win_ref[SLAB_B_B * CHUNKS_PER_SLAB + c])
        cbuf[HALO:HALO + tm, chunk_cols(c)] = a * _sigmoid(b)
    t_glu = [add(mxu, 1024 * scale, functools.partial(glu_emit, c), t_rms) for c in range(CHUNKS_PER_SLAB)]

    t_av = slab(z_av, SLAB_A_V, _gelu_tanh)

    def v_body(r):
        v_s[r:r + ROWS, :] = _layer_norm_rows(rows(z_av, r), lag_ref[...], lab_ref[...]).astype(_BF16)
    t_v = row_phase(50, v_body, t_av)

    def spatial_body(g):
        tri = (lax.broadcasted_iota(jnp.int32, (CHUNK, CHUNK), 0)
               >= lax.broadcasted_iota(jnp.int32, (CHUNK, CHUNK), 1))
        w_g = jnp.where(tri, ws_ref[g], jnp.zeros((), _BF16))
        cols = slice(g * GROUP_DIM, (g + 1) * GROUP_DIM)
        for c in range(0, tm, 2 * CHUNK):
            pair = jnp.concatenate([v_s[c:c + CHUNK, cols], v_s[c + CHUNK:c + 2 * CHUNK, cols]], axis=1)
            mixed = _dot(w_g, pair)
            sv_s[c:c + CHUNK, cols] = mixed[:, :GROUP_DIM]
            sv_s[c + CHUNK:c + 2 * CHUNK, cols] = mixed[:, GROUP_DIM:]
    t_sp = [add(mxu, 50 * (tm // CHUNK), functools.partial(spatial_body, g), t_v) for g in range(GROUPS)]

    t_au, t_ag = slab(z_au, SLAB_A_U, _gelu_tanh), slab(z_ag, SLAB_A_G, _silu)

    def bra_body(r):
        rc = r % CHUNK
        sv = rows(sv_s, r) + bs_ref[rc:rc + ROWS, :]
        bra_s[r:r + ROWS, :] = (rows(z_au, r) * sv * rows(z_ag, r)).astype(_BF16)
    t_bra = row_phase(30, bra_body, t_sp + t_au + t_ag)

    first = HALO - (CONV_WIDTH - 1)

    def conv_body(r, j, pin=None):
        y = None
        for b in range(SUBLANES):
            n = CONV_ROWS + (SUBLANES if b else 0)
            u = None
            for k in range(CONV_WIDTH):
                off = first + k
                if off % SUBLANES != b:
                    continue
                a = off - b
                w_k = cw_ref[k:k + 1, j:j + LANES]
                if pin is not None and u is None and b == 0:
                    w_k = jnp.where(pl.program_id(0) < 0, pin()[0:1, :], w_k)
                term = cbuf[r + a:r + a + n, j:j + LANES] * w_k
                u = term if u is None else u + term
            if b:
                u = pltpu.roll(u, n - b, axis=0)[0:CONV_ROWS, :]
            y = u if y is None else y + u
        y_s[r:r + CONV_ROWS, j:j + LANES] = y

    t_m0 = slab(z_m0, SLAB_MERGE + 0, _sigmoid)
    t_m1 = slab(z_m1, SLAB_MERGE + 1, _sigmoid)
    t_m2 = slab(z_m2, SLAB_MERGE + 2, _sigmoid)
    t_cg = slab(z_cg, SLAB_C_G, _silu)
    pinned = [(t, functools.partial(lambda buf, c: buf[0:SUBLANES, c * MXU_COLS:c * MXU_COLS + LANES], buf, c))
              for buf, tasks in ((z_m0, t_m0), (z_m1, t_m1), (z_m2, t_m2), (z_cg, t_cg))
              for c, t in enumerate(tasks)]
    blocks = [(r, j) for r in range(0, tm, CONV_ROWS) for j in range(0, WIDTH, LANES)]
    first_pinned, pin_every = 8, 3
    t_conv = []
    for i, (r, j) in enumerate(blocks):
        deps, pin = [t_glu[j // MXU_COLS]], None
        if i >= first_pinned and (i - first_pinned) % pin_every == 0 and (i - first_pinned) // pin_every < len(pinned):
            task, pin = pinned[(i - first_pinned) // pin_every]
            deps.append(task)
        t_conv.append(add(vpu, 170, functools.partial(conv_body, r, j, pin), deps))

    def carry_body():
        cbuf[0:HALO, :] = cbuf[tm:tm + HALO, :]
    add(vpu, 10, carry_body, t_conv)

    t_bg = slab(z_bg, SLAB_B_G, _silu)

    def brb_body(r):
        c = _silu(_layer_norm_rows(rows(y_s, r) + cb_ref[...], lbg_ref[...], lbb_ref[...]))
        brb_s[r:r + ROWS, :] = (c * rows(z_bg, r)).astype(_BF16)
    t_brb = row_phase(80, brb_body, t_conv + t_bg)

    def w_branch_chunk(n):
        return lambda c: wb_ref[n * CHUNKS_PER_SLAB + c]

    def merge_a(c, p):
        acc_s[:, chunk_cols(c)] = z_m0[:, chunk_cols(c)] * p
    t_pa = matmul_chunks(bra_s, w_branch_chunk(0), t_bra + t_m0, merge_a)


    def q_finish(c, z):
        q_s[:, chunk_cols(c)] = z.astype(_BF16)
    t_cq = matmul_chunks(h_s, w_in_chunk(SLAB_C_Q), t_rms, q_finish)
    nt_dims = (((1,), (1,)), ((), ()))

    def scores_body(hd):
        cols = slice(hd * HEAD_DIM, (hd + 1) * HEAD_DIM)
        sv_s[:, hd * MEM_LEN:(hd + 1) * MEM_LEN] = lax.dot_general(
            q_s[:, cols], k_ref[:, cols], nt_dims, preferred_element_type=_F32)
    t_sc = [add(mxu, 100 * scale, functools.partial(scores_body, hd), t_cq + t_bra) for hd in range(HEADS)]

    def softmax_body(r):
        for hd in range(HEADS):
            cols = slice(hd * MEM_LEN, (hd + 1) * MEM_LEN)
            s = sv_s[r:r + ROWS, cols]
            p = jnp.exp(s - jnp.max(s, axis=-1, keepdims=True))
            v_s[r:r + ROWS, cols] = (p * (1.0 / jnp.sum(p, axis=-1, keepdims=True))).astype(_BF16)
    t_sm = row_phase(40, softmax_body, t_sc + t_sp)

    def pv_body(hd):
        cols = slice(hd * HEAD_DIM, (hd + 1) * HEAD_DIM)
        att = _dot(v_s[:, hd * MEM_LEN:(hd + 1) * MEM_LEN], v_ref[:, cols])
        bra_s[:, cols] = (att * z_cg[:, cols]).astype(_BF16)
    t_pv = [add(mxu, 100 * scale, functools.partial(pv_body, hd), t_sm + t_cg + t_pa) for hd in range(HEADS)]

    def merge_c(c, p):
        acc_s[:, chunk_cols(c)] = acc_s[:, chunk_cols(c)] + z_m2[:, chunk_cols(c)] * p
    t_pc = matmul_chunks(bra_s, w_branch_chunk(2), t_pv + t_m2 + t_pa, merge_c)

    def merge_b(c, p):
        merged = acc_s[:, chunk_cols(c)] + z_m1[:, chunk_cols(c)] * p
        bra_s[:, chunk_cols(c)] = merged.astype(_BF16)
    t_pb = matmul_chunks(brb_s, w_branch_chunk(1), t_brb + t_m1 + t_pc, merge_b)

    if final:
        def out_finish(c, p):
            y_s[:, chunk_cols(c)] = x_ref[:, chunk_cols(c)] + p
        t_po = matmul_chunks(bra_s, lambda c: wo_ref[c], t_pb, out_finish)

        def out_body(r):
            o_ref[r:r + ROWS, :] = _rms_norm_rows(rows(y_s, r), fg_ref[...])
        row_phase(30, out_body, t_po)
    else:
        def out_finish(c, p):
            o_ref[:, chunk_cols(c)] = x_ref[:, chunk_cols(c)] + p
        t_po = matmul_chunks(bra_s, lambda c: wo_ref[c], t_pb, out_finish)

    mxu_order = (t_sp + t_pa + t_sc + t_pv + t_pc + t_pb + t_po
                 + t_glu + t_av + t_au + t_ag + t_cq + t_bg + t_m0 + t_cg + t_m1 + t_m2)
    assert len(mxu_order) == len(mxu)
    _trace_two_streams((mxu_order, vpu))


def _layer(x, k_all, v_all, layer, final, norm_g, w_in, lag, lab, w_s, bs_full, conv_w, conv_b,
           lbg, lbb, w_branch, w_out, final_g):
    batch, seq, _ = x.shape
    tm = TILE_TOKENS
    assert seq % tm == 0 and tm % CHUNK == 0 and tm % CONV_ROWS == 0

    def resident(shape):
        nd = len(shape)
        return pl.BlockSpec((None,) + shape, lambda b, s: (layer,) + (0,) * nd,
                            pipeline_mode=pl.Buffered(1))

    tile = pl.BlockSpec((None, tm, D_MODEL), lambda b, s: (b, s, 0))
    kv_spec = pl.BlockSpec((None, None, MEM_LEN, WIDTH), lambda b, s: (layer, b, 0, 0))
    row = (1, WIDTH)
    f32_tile = pltpu.VMEM((tm, WIDTH), _F32)
    bf16_tile = pltpu.VMEM((tm, WIDTH), _BF16)
    return pl.pallas_call(
        functools.partial(_layer_kernel, tm=tm, final=final),
        grid=(batch, seq // tm),
        in_specs=[
            tile, kv_spec, kv_spec,
            resident(row),
            resident((N_SLABS * CHUNKS_PER_SLAB, D_MODEL, MXU_COLS)),
            resident(row), resident(row),
            resident((GROUPS, CHUNK, CHUNK)),
            resident((CHUNK, WIDTH)),
            resident((CONV_WIDTH, WIDTH)),
            resident(row), resident(row), resident(row),
            resident((3 * CHUNKS_PER_SLAB, WIDTH, MXU_COLS)),
            resident((CHUNKS_PER_SLAB, D_MODEL, MXU_COLS)),
            pl.BlockSpec((1, D_MODEL), lambda b, s: (0, 0), pipeline_mode=pl.Buffered(1)),
        ],
        out_specs=tile,
        out_shape=jax.ShapeDtypeStruct(x.shape, x.dtype),
        scratch_shapes=[bf16_tile] * 5 + [f32_tile] * 11 + [pltpu.VMEM((tm + HALO, WIDTH), _F32)],
        compiler_params=pltpu.CompilerParams(
            dimension_semantics=("arbitrary", "arbitrary"),
            vmem_limit_bytes=VMEM_LIMIT_BYTES),
        name=f"trunk_layer_{layer}",
    )(x, k_all, v_all, norm_g, w_in, lag, lab, w_s, bs_full, conv_w, conv_b, lbg, lbb,
      w_branch, w_out, final_g)


def _chunk_kernel(w_ref, o_ref):
    for i in range(PREP_COLS // MXU_COLS):
        o_ref[i] = w_ref[:, i * MXU_COLS:(i + 1) * MXU_COLS].astype(_BF16)


def _column_chunks(w):
    n_mats, k, n = w.shape
    per_block = PREP_COLS // MXU_COLS
    return pl.pallas_call(
        _chunk_kernel,
        grid=(n_mats, n // PREP_COLS),
        in_specs=[pl.BlockSpec((None, k, PREP_COLS), lambda l, j: (l, 0, j))],
        out_specs=pl.BlockSpec((None, per_block, k, MXU_COLS), lambda l, j: (l, j, 0, 0)),
        out_shape=jax.ShapeDtypeStruct((n_mats, n // MXU_COLS, k, MXU_COLS), _BF16),
        compiler_params=pltpu.CompilerParams(dimension_semantics=("arbitrary", "arbitrary")),
        name="weight_chunks",
    )(w)


def kernel(x, mem, norm_g, mem_norm_g, w_in, gmlp_ln_g, gmlp_ln_b, w_s, b_s, conv_w, conv_b,
           conv_ln_g, conv_ln_b, w_kv, w_branch, w_out, final_norm_g):
    depth = w_in.shape[0]
    vec = lambda a: a.reshape(depth, 1, a.shape[-1])
    bs_full = jnp.repeat(jnp.swapaxes(b_s, 1, 2), GROUP_DIM, axis=2)
    k_all, v_all = _keys_values(mem, mem_norm_g, w_kv)
    w_s_b = w_s.astype(_BF16)
    w_in_b = _column_chunks(w_in)
    w_branch_b = _column_chunks(w_branch.reshape(-1, WIDTH, D_MODEL)).reshape(depth, -1, WIDTH, MXU_COLS)
    w_out_b = _column_chunks(w_out)
    final_g = final_norm_g.reshape(1, D_MODEL)
    for layer in range(depth):
        x = _layer(x, k_all, v_all, layer, layer == depth - 1, vec(norm_g), w_in_b,
                   vec(gmlp_ln_g), vec(gmlp_ln_b), w_s_b, bs_full, conv_w, vec(conv_b),
                   vec(conv_ln_g), vec(conv_ln_b), w_branch_b, w_out_b, final_g)
    return x
```

```python
import functools

import jax
import jax.numpy as jnp
from jax import lax
from jax.experimental import pallas as pl
from jax.experimental.pallas import tpu as pltpu

D_MODEL = 1024
WIDTH = 1024
CHUNK = 128
GROUPS = 8
GROUP_DIM = WIDTH // GROUPS
CONV_WIDTH = 31
HEADS = 4
HEAD_DIM = WIDTH // HEADS
MEM_LEN = 256
N_SLABS = 11
N_IN = N_SLABS * WIDTH
RMS_EPS = 1e-6
LN_EPS = 1e-5

SLAB_A_U, SLAB_A_V, SLAB_A_G, SLAB_B_A, SLAB_B_B, SLAB_B_G, SLAB_C_Q, SLAB_C_G = range(8)
SLAB_MERGE = 8

LANES = 128
SUBLANES = 8
ROWS = 16
MXU_COLS = 512
CHUNKS_PER_SLAB = WIDTH // MXU_COLS
PREP_COLS = 1024
HALO = 32
CONV_ROWS = 64
TILE_TOKENS = 256
TILES_PER_STEP = 2
VMEM_LIMIT_BYTES = 58 * 1024 * 1024

_BF16 = jnp.bfloat16
_F32 = jnp.float32

_GELU_C0 = 0.7978845608028654
_GELU_C1 = _GELU_C0 * 0.044715


def _gelu_tanh(x):
    hx = 0.5 * x
    return hx + hx * jnp.tanh(x * (_GELU_C0 + _GELU_C1 * (x * x)))


def _sigmoid(x):
    return 0.5 + 0.5 * jnp.tanh(0.5 * x)


def _silu(x):
    hx = 0.5 * x
    return hx + hx * jnp.tanh(hx)


def _layer_norm_rows(x, g, b):
    mu = jnp.mean(x, axis=-1, keepdims=True)
    xc = x - mu
    var = jnp.mean(xc * xc, axis=-1, keepdims=True)
    return xc * lax.rsqrt(var + LN_EPS) * g + b


def _rms_norm_rows(x, g):
    return x * lax.rsqrt(jnp.mean(x * x, axis=-1, keepdims=True) + RMS_EPS) * g


def _dot(a, b):
    return jnp.dot(a, b, preferred_element_type=_F32)


def _kv_kernel(mem_ref, g_ref, wkv_ref, k_ref, v_ref):
    mem_n = _rms_norm_rows(mem_ref[...], g_ref[...])
    kv = _dot(mem_n.astype(_BF16), wkv_ref[...].astype(_BF16))
    k_ref[...] = (kv[:, :WIDTH] * (HEAD_DIM ** -0.5)).astype(_BF16)
    v_ref[...] = kv[:, WIDTH:].astype(_BF16)


def _keys_values(mem, mem_norm_g, w_kv):
    depth = w_kv.shape[0]
    batch = mem.shape[0]
    out = jax.ShapeDtypeStruct((depth, batch, MEM_LEN, WIDTH), _BF16)
    kv_spec = pl.BlockSpec((None, None, MEM_LEN, WIDTH), lambda l, b: (l, b, 0, 0))
    return pl.pallas_call(
        _kv_kernel,
        grid=(depth, batch),
        in_specs=[
            pl.BlockSpec((None, MEM_LEN, D_MODEL), lambda l, b: (b, 0, 0)),
            pl.BlockSpec((None, 1, D_MODEL), lambda l, b: (l, 0, 0)),
            pl.BlockSpec((None, D_MODEL, 2 * WIDTH), lambda l, b: (l, 0, 0)),
        ],
        out_specs=[kv_spec, kv_spec],
        out_shape=[out, out],
        compiler_params=pltpu.CompilerParams(dimension_semantics=("arbitrary", "arbitrary"),
                                             vmem_limit_bytes=VMEM_LIMIT_BYTES),
        name="keys_values",
    )(mem, mem_norm_g.reshape(depth, 1, D_MODEL), w_kv)


class _Task:
    def __init__(self, cost, emit, deps=()):
        self.cost, self.emit, self.deps, self.finish = cost, emit, tuple(deps), None


def _trace_two_streams(streams):
    pending = [list(s) for s in streams]
    unit_free = [0.0, 0.0]
    while any(pending):
        best = None
        for u, tasks in enumerate(pending):
            for task in tasks:
                if all(d.finish is not None for d in task.deps):
                    start = max([unit_free[u]] + [d.finish for d in task.deps])
                    if best is None or start < best[0]:
                        best = (start, u, task)
                    if start <= unit_free[u]:
                        break
        assert best is not None, "task graph has a cycle"
        start, u, task = best
        task.emit()
        task.finish = start + task.cost
        unit_free[u] = task.finish
        pending[u].remove(task)


def _layer_kernel(x_ref, *refs, tm, final):
    o_ref = refs[15]

    def tile(t, carry):
        rows_t = pl.ds(pl.multiple_of(t * tm, tm), tm)
        _tile(x_ref.at[rows_t, :], *refs[:15], o_ref.at[rows_t, :], *refs[16:], tm=tm, final=final,
              first_tile=(pl.program_id(1) == 0) & (t == 0))
        return carry
    lax.fori_loop(0, TILES_PER_STEP, tile, 0)


def _tile(x_ref, k_ref, v_ref, ng_ref, win_ref, lag_ref, lab_ref, ws_ref, bs_ref,
          cw_ref, cb_ref, lbg_ref, lbb_ref, wb_ref, wo_ref, fg_ref,
          o_ref,
          h_s, v_s, bra_s, brb_s, q_s,
          z_au, z_av, z_ag, z_bg, z_cg, z_m0, z_m1, z_m2,
          y_s, sv_s, acc_s, cbuf, *, tm, final, first_tile):
    scale = tm / 256.0
    mxu, vpu = [], []
    row_blocks = range(0, tm, ROWS)

    def rows(ref, r):
        return ref[r:r + ROWS, :]

    def add(stream, cost, emit, deps=()):
        task = _Task(cost, emit, deps)
        stream.append(task)
        return task

    def chunk_cols(c):
        return slice(c * MXU_COLS, (c + 1) * MXU_COLS)

    def matmul_chunks(lhs_ref, rhs, deps, finish):
        def emit(c):
            finish(c, _dot(lhs_ref[...], rhs(c)))
        return [add(mxu, 512 * scale, functools.partial(emit, c), deps) for c in range(CHUNKS_PER_SLAB)]

    def row_phase(cost, body, deps):
        return [add(vpu, cost, functools.partial(body, r), deps) for r in row_blocks]

    @pl.when(first_tile)
    def _():
        cbuf[0:HALO, :] = jnp.zeros((HALO, WIDTH), _F32)

    def rms_body(r):
        h_s[r:r + ROWS, :] = _rms_norm_rows(rows(x_ref, r), ng_ref[...]).astype(_BF16)
    t_rms = row_phase(30, rms_body, ())

    def w_in_chunk(j):
        return lambda c: win_ref[j * CHUNKS_PER_SLAB + c]

    def slab(dst, j, act):
        def finish(c, z):
            dst[:, chunk_cols(c)] = act(z).astype(dst.dtype)
        return matmul_chunks(h_s, w_in_chunk(j), t_rms, finish)

    def glu_emit(c):
        a = _dot(h_s[...], win_ref[SLAB_B_A * CHUNKS_PER_SLAB + c])
        b = _dot(h_s[...], win_ref[SLAB_B_B * CHUNKS_PER_SLAB + c])
        cbuf[HALO:HALO + tm, chunk_cols(c)] = a * _sigmoid(b)
    t_glu = [add(mxu, 1024 * scale, functools.partial(glu_emit, c), t_rms) for c in range(CHUNKS_PER_SLAB)]

    t_av = slab(z_av, SLAB_A_V, _gelu_tanh)

    def v_body(r):
        v_s[r:r + ROWS, :] = _layer_norm_rows(rows(z_av, r), lag_ref[...], lab_ref[...]).astype(_BF16)
    t_v = row_phase(50, v_body, t_av)

    def spatial_body(g):
        tri = (lax.broadcasted_iota(jnp.int32, (CHUNK, CHUNK), 0)
               >= lax.broadcasted_iota(jnp.int32, (CHUNK, CHUNK), 1))
        w_g = jnp.where(tri, ws_ref[g], jnp.zeros((), _BF16))
        cols = slice(g * GROUP_DIM, (g + 1) * GROUP_DIM)
        for c in range(0, tm, 2 * CHUNK):
            pair = jnp.concatenate([v_s[c:c + CHUNK, cols], v_s[c + CHUNK:c + 2 * CHUNK, cols]], axis=1)
            mixed = _dot(w_g, pair)
            sv_s[c:c + CHUNK, cols] = mixed[:, :GROUP_DIM]
            sv_s[c + CHUNK:c + 2 * CHUNK, cols] = mixed[:, GROUP_DIM:]
    t_sp = [add(mxu, 50 * (tm // CHUNK), functools.partial(spatial_body, g), t_v) for g in range(GROUPS)]

    t_au, t_ag = slab(z_au, SLAB_A_U, _gelu_tanh), slab(z_ag, SLAB_A_G, _silu)

    def bra_body(r):
        rc = r % CHUNK
        sv = rows(sv_s, r) + bs_ref[rc:rc + ROWS, :]
        bra_s[r:r + ROWS, :] = (rows(z_au, r) * sv * rows(z_ag, r)).astype(_BF16)
    t_bra = row_phase(30, bra_body, t_sp + t_au + t_ag)

    first = HALO - (CONV_WIDTH - 1)

    def conv_body(r, j, pin=None):
        y = None
        for b in range(SUBLANES):
            n = CONV_ROWS + (SUBLANES if b else 0)
            u = None
            for k in range(CONV_WIDTH):
                off = first + k
                if off % SUBLANES != b:
                    continue
                a = off - b
                w_k = cw_ref[k:k + 1, j:j + LANES]
                if pin is not None and u is None and b == 0:
                    w_k = jnp.where(pl.program_id(0) < 0, pin()[0:1, :], w_k)
                term = cbuf[r + a:r + a + n, j:j + LANES] * w_k
                u = term if u is None else u + term
            if b:
                u = pltpu.roll(u, n - b, axis=0)[0:CONV_ROWS, :]
            y = u if y is None else y + u
        y_s[r:r + CONV_ROWS, j:j + LANES] = y

    t_m0 = slab(z_m0, SLAB_MERGE + 0, _sigmoid)
    t_m1 = slab(z_m1, SLAB_MERGE + 1, _sigmoid)
    t_m2 = slab(z_m2, SLAB_MERGE + 2, _sigmoid)
    t_cg = slab(z_cg, SLAB_C_G, _silu)
    pinned = [(t, functools.partial(lambda buf, c: buf[0:SUBLANES, c * MXU_COLS:c * MXU_COLS + LANES], buf, c))
              for buf, tasks in ((z_m0, t_m0), (z_m1, t_m1), (z_m2, t_m2), (z_cg, t_cg))
              for c, t in enumerate(tasks)]
    blocks = [(r, j) for r in range(0, tm, CONV_ROWS) for j in range(0, WIDTH, LANES)]
    first_pinned, pin_every = 8, 3
    t_conv = []
    for i, (r, j) in enumerate(blocks):
        deps, pin = [t_glu[j // MXU_COLS]], None
        if i >= first_pinned and (i - first_pinned) % pin_every == 0 and (i - first_pinned) // pin_every < len(pinned):
            task, pin = pinned[(i - first_pinned) // pin_every]
            deps.append(task)
        t_conv.append(add(vpu, 170, functools.partial(conv_body, r, j, pin), deps))

    def carry_body():
        cbuf[0:HALO, :] = cbuf[tm:tm + HALO, :]
    add(vpu, 10, carry_body, t_conv)

    t_bg = slab(z_bg, SLAB_B_G, _silu)

    def brb_body(r):
        c = _silu(_layer_norm_rows(rows(y_s, r) + cb_ref[...], lbg_ref[...], lbb_ref[...]))
        brb_s[r:r + ROWS, :] = (c * rows(z_bg, r)).astype(_BF16)
    t_brb = row_phase(80, brb_body, t_conv + t_bg)

    def w_branch_chunk(n):
        return lambda c: wb_ref[n * CHUNKS_PER_SLAB + c]

    def merge_a(c, p):
        acc_s[:, chunk_cols(c)] = z_m0[:, chunk_cols(c)] * p
    t_pa = matmul_chunks(bra_s, w_branch_chunk(0), t_bra + t_m0, merge_a)


    def q_finish(c, z):
        q_s[:, chunk_cols(c)] = z.astype(_BF16)
    t_cq = matmul_chunks(h_s, w_in_chunk(SLAB_C_Q), t_rms, q_finish)
    nt_dims = (((1,), (1,)), ((), ()))

    def scores_body(hd):
        cols = slice(hd * HEAD_DIM, (hd + 1) * HEAD_DIM)
        sv_s[:, hd * MEM_LEN:(hd + 1) * MEM_LEN] = lax.dot_general(
            q_s[:, cols], k_ref[:, cols], nt_dims, preferred_element_type=_F32)
    t_sc = [add(mxu, 100 * scale, functools.partial(scores_body, hd), t_cq + t_bra) for hd in range(HEADS)]

    def softmax_body(r):
        for hd in range(HEADS):
            cols = slice(hd * MEM_LEN, (hd + 1) * MEM_LEN)
            s = sv_s[r:r + ROWS, cols]
            p = jnp.exp(s - jnp.max(s, axis=-1, keepdims=True))
            v_s[r:r + ROWS, cols] = (p * (1.0 / jnp.sum(p, axis=-1, keepdims=True))).astype(_BF16)
    t_sm = row_phase(40, softmax_body, t_sc + t_sp)

    def pv_body(hd):
        cols = slice(hd * HEAD_DIM, (hd + 1) * HEAD_DIM)
        att = _dot(v_s[:, hd * MEM_LEN:(hd + 1) * MEM_LEN], v_ref[:, cols])
        bra_s[:, cols] = (att * z_cg[:, cols]).astype(_BF16)
    t_pv = [add(mxu, 100 * scale, functools.partial(pv_body, hd), t_sm + t_cg + t_pa) for hd in range(HEADS)]

    def merge_c(c, p):
        acc_s[:, chunk_cols(c)] = acc_s[:, chunk_cols(c)] + z_m2[:, chunk_cols(c)] * p
    t_pc = matmul_chunks(bra_s, w_branch_chunk(2), t_pv + t_m2 + t_pa, merge_c)

    def merge_b(c, p):
        merged = acc_s[:, chunk_cols(c)] + z_m1[:, chunk_cols(c)] * p
        bra_s[:, chunk_cols(c)] = merged.astype(_BF16)
    t_pb = matmul_chunks(brb_s, w_branch_chunk(1), t_brb + t_m1 + t_pc, merge_b)

    if final:
        def out_finish(c, p):
            y_s[:, chunk_cols(c)] = x_ref[:, chunk_cols(c)] + p
        t_po = matmul_chunks(bra_s, lambda c: wo_ref[c], t_pb, out_finish)

        def out_body(r):
            o_ref[r:r + ROWS, :] = _rms_norm_rows(rows(y_s, r), fg_ref[...])
        row_phase(30, out_body, t_po)
    else:
        def out_finish(c, p):
            o_ref[:, chunk_cols(c)] = x_ref[:, chunk_cols(c)] + p
        t_po = matmul_chunks(bra_s, lambda c: wo_ref[c], t_pb, out_finish)

    mxu_order = (t_sp + t_pa + t_sc + t_pv + t_pc + t_pb + t_po
                 + t_glu + t_av + t_au + t_ag + t_cq + t_bg + t_m0 + t_cg + t_m1 + t_m2)
    assert len(mxu_order) == len(mxu)
    _trace_two_streams((mxu_order, vpu))


def _layer(x, k_all, v_all, layer, final, norm_g, w_in, lag, lab, w_s, bs_full, conv_w, conv_b,
           lbg, lbb, w_branch, w_out, final_g):
    batch, seq, _ = x.shape
    tm = TILE_TOKENS
    step = TILES_PER_STEP * tm
    assert seq % step == 0 and tm % CHUNK == 0 and tm % CONV_ROWS == 0

    def resident(shape):
        nd = len(shape)
        return pl.BlockSpec((None,) + shape, lambda b, s: (layer,) + (0,) * nd,
                            pipeline_mode=pl.Buffered(1))

    tile = pl.BlockSpec((None, step, D_MODEL), lambda b, s: (b, s, 0))
    kv_spec = pl.BlockSpec((None, None, MEM_LEN, WIDTH), lambda b, s: (layer, b, 0, 0))
    row = (1, WIDTH)
    f32_tile = pltpu.VMEM((tm, WIDTH), _F32)
    bf16_tile = pltpu.VMEM((tm, WIDTH), _BF16)
    return pl.pallas_call(
        functools.partial(_layer_kernel, tm=tm, final=final),
        grid=(batch, seq // step),
        in_specs=[
            tile, kv_spec, kv_spec,
            resident(row),
            resident((N_SLABS * CHUNKS_PER_SLAB, D_MODEL, MXU_COLS)),
            resident(row), resident(row),
            resident((GROUPS, CHUNK, CHUNK)),
            resident((CHUNK, WIDTH)),
            resident((CONV_WIDTH, WIDTH)),
            resident(row), resident(row), resident(row),
            resident((3 * CHUNKS_PER_SLAB, WIDTH, MXU_COLS)),
            resident((CHUNKS_PER_SLAB, D_MODEL, MXU_COLS)),
            pl.BlockSpec((1, D_MODEL), lambda b, s: (0, 0), pipeline_mode=pl.Buffered(1)),
        ],
        out_specs=tile,
        out_shape=jax.ShapeDtypeStruct(x.shape, x.dtype),
        scratch_shapes=[bf16_tile] * 5 + [f32_tile] * 11 + [pltpu.VMEM((tm + HALO, WIDTH), _F32)],
        compiler_params=pltpu.CompilerParams(
            dimension_semantics=("arbitrary", "arbitrary"),
            vmem_limit_bytes=VMEM_LIMIT_BYTES),
        name=f"trunk_layer_{layer}",
    )(x, k_all, v_all, norm_g, w_in, lag, lab, w_s, bs_full, conv_w, conv_b, lbg, lbb,
      w_branch, w_out, final_g)


def _chunk_kernel(w_ref, o_ref):
    for i in range(PREP_COLS // MXU_COLS):
        o_ref[i] = w_ref[:, i * MXU_COLS:(i + 1) * MXU_COLS].astype(_BF16)


def _column_chunks(w):
    n_mats, k, n = w.shape
    per_block = PREP_COLS // MXU_COLS
    return pl.pallas_call(
        _chunk_kernel,
        grid=(n_mats, n // PREP_COLS),
        in_specs=[pl.BlockSpec((None, k, PREP_COLS), lambda l, j: (l, 0, j))],
        out_specs=pl.BlockSpec((None, per_block, k, MXU_COLS), lambda l, j: (l, j, 0, 0)),
        out_shape=jax.ShapeDtypeStruct((n_mats, n // MXU_COLS, k, MXU_COLS), _BF16),
        compiler_params=pltpu.CompilerParams(dimension_semantics=("arbitrary", "arbitrary")),
        name="weight_chunks",
    )(w)


def kernel(x, mem, norm_g, mem_norm_g, w_in, gmlp_ln_g, gmlp_ln_b, w_s, b_s, conv_w, conv_b,
           conv_ln_g, conv_ln_b, w_kv, w_branch, w_out, final_norm_g):
    depth = w_in.shape[0]
    vec = lambda a: a.reshape(depth, 1, a.shape[-1])
    bs_full = jnp.repeat(jnp.swapaxes(b_s, 1, 2), GROUP_DIM, axis=2)
    k_all, v_all = _keys_values(mem, mem_norm_g, w_kv)
    w_s_b = w_s.astype(_BF16)
    w_in_b = _column_chunks(w_in)
    w_branch_b = _column_chunks(w_branch.reshape(-1, WIDTH, D_MODEL)).reshape(depth, -1, WIDTH, MXU_COLS)
    w_out_b = _column_chunks(w_out)
    final_g = final_norm_g.reshape(1, D_MODEL)
    for layer in range(depth):
        x = _layer(x, k_all, v_all, layer, layer == depth - 1, vec(norm_g), w_in_b,
                   vec(gmlp_ln_g), vec(gmlp_ln_b), w_s_b, bs_full, conv_w, vec(conv_b),
                   vec(conv_ln_g), vec(conv_ln_b), w_branch_b, w_out_b, final_g)
    return x
```

```python
import functools

import jax
import jax.numpy as jnp
from jax import lax
from jax.experimental import pallas as pl
from jax.experimental.pallas import tpu as pltpu

D_MODEL = 1024
WIDTH = 1024
CHUNK = 128
GROUPS = 8
GROUP_DIM = WIDTH // GROUPS
CONV_WIDTH = 31
HEADS = 4
HEAD_DIM = WIDTH // HEADS
MEM_LEN = 256
N_SLABS = 11
N_IN = N_SLABS * WIDTH
RMS_EPS = 1e-6
LN_EPS = 1e-5

SLAB_A_U, SLAB_A_V, SLAB_A_G, SLAB_B_A, SLAB_B_B, SLAB_B_G, SLAB_C_Q, SLAB_C_G = range(8)
SLAB_MERGE = 8

LANES = 128
SUBLANES = 8
ROWS = 16
MXU_COLS = 512
CHUNKS_PER_SLAB = WIDTH // MXU_COLS
PREP_COLS = 1024
HALO = 32
CONV_ROWS = 64
TILE_TOKENS = 256
TILES_PER_STEP = 2
VMEM_LIMIT_BYTES = 58 * 1024 * 1024

_BF16 = jnp.bfloat16
_F32 = jnp.float32

_GELU_C0 = 0.7978845608028654
_GELU_C1 = _GELU_C0 * 0.044715


def _gelu_tanh(x):
    hx = 0.5 * x
    return hx + hx * jnp.tanh(x * (_GELU_C0 + _GELU_C1 * (x * x)))


def _sigmoid(x):
    return 0.5 + 0.5 * jnp.tanh(0.5 * x)


def _silu(x):
    hx = 0.5 * x
    return hx + hx * jnp.tanh(hx)


def _layer_norm_rows(x, g, b):
    mu = jnp.mean(x, axis=-1, keepdims=True)
    xc = x - mu
    var = jnp.mean(xc * xc, axis=-1, keepdims=True)
    return xc * lax.rsqrt(var + LN_EPS) * g + b


def _rms_norm_rows(x, g):
    return x * lax.rsqrt(jnp.mean(x * x, axis=-1, keepdims=True) + RMS_EPS) * g


def _dot(a, b):
    return jnp.dot(a, b, preferred_element_type=_F32)


def _kv_kernel(mem_ref, g_ref, wkv_ref, k_ref, v_ref):
    mem_n = _rms_norm_rows(mem_ref[...], g_ref[...])
    kv = _dot(mem_n.astype(_BF16), wkv_ref[...].astype(_BF16))
    k_ref[...] = (kv[:, :WIDTH] * (HEAD_DIM ** -0.5)).astype(_BF16)
    v_ref[...] = kv[:, WIDTH:].astype(_BF16)


def _keys_values(mem, mem_norm_g, w_kv):
    depth = w_kv.shape[0]
    batch = mem.shape[0]
    out = jax.ShapeDtypeStruct((depth, batch, MEM_LEN, WIDTH), _BF16)
    kv_spec = pl.BlockSpec((None, None, MEM_LEN, WIDTH), lambda l, b: (l, b, 0, 0))
    return pl.pallas_call(
        _kv_kernel,
        grid=(depth, batch),
        in_specs=[
            pl.BlockSpec((None, MEM_LEN, D_MODEL), lambda l, b: (b, 0, 0)),
            pl.BlockSpec((None, 1, D_MODEL), lambda l, b: (l, 0, 0)),
            pl.BlockSpec((None, D_MODEL, 2 * WIDTH), lambda l, b: (l, 0, 0)),
        ],
        out_specs=[kv_spec, kv_spec],
        out_shape=[out, out],
        compiler_params=pltpu.CompilerParams(dimension_semantics=("arbitrary", "arbitrary"),
                                             vmem_limit_bytes=VMEM_LIMIT_BYTES),
        name="keys_values",
    )(mem, mem_norm_g.reshape(depth, 1, D_MODEL), w_kv)


class _Task:
    def __init__(self, cost, emit, deps=()):
        self.cost, self.emit, self.deps, self.finish = cost, emit, tuple(deps), None


def _trace_two_streams(streams):
    pending = [list(s) for s in streams]
    unit_free = [0.0, 0.0]
    while any(pending):
        best = None
        for u, tasks in enumerate(pending):
            for task in tasks:
                if all(d.finish is not None for d in task.deps):
                    start = max([unit_free[u]] + [d.finish for d in task.deps])
                    if best is None or start < best[0]:
                        best = (start, u, task)
                    if start <= unit_free[u]:
                        break
        assert best is not None, "task graph has a cycle"
        start, u, task = best
        task.emit()
        task.finish = start + task.cost
        unit_free[u] = task.finish
        pending[u].remove(task)


def _layer_kernel(x_ref, *refs, tm, final):
    o_ref = refs[15]

    def tile(t, carry):
        rows_t = pl.ds(pl.multiple_of(t * tm, tm), tm)
        _tile(x_ref.at[rows_t, :], *refs[:15], o_ref.at[rows_t, :], *refs[16:], tm=tm, final=final,
              first_tile=(pl.program_id(1) == 0) & (t == 0))
        return carry
    lax.fori_loop(0, TILES_PER_STEP, tile, 0)


def _tile(x_ref, k_ref, v_ref, ng_ref, win_ref, lag_ref, lab_ref, ws_ref, bs_ref,
          cw_ref, cb_ref, lbg_ref, lbb_ref, wb_ref, wo_ref, fg_ref,
          o_ref,
          z_au, z_av, z_ag, z_bg, z_cg, z_m0, z_m1, z_m2,
          y_s, sv_s, acc_s,
          h_s, v_s, bra_s, brb_s, q_s, cbuf, *, tm, final, first_tile):
    scale = tm / 256.0
    mxu, vpu = [], []
    row_blocks = range(0, tm, ROWS)

    def rows(ref, r):
        return ref[r:r + ROWS, :]

    def add(stream, cost, emit, deps=()):
        task = _Task(cost, emit, deps)
        stream.append(task)
        return task

    def chunk_cols(c):
        return slice(c * MXU_COLS, (c + 1) * MXU_COLS)

    def matmul_chunks(lhs_ref, rhs, deps, finish):
        def emit(c):
            finish(c, _dot(lhs_ref[...], rhs(c)))
        return [add(mxu, 512 * scale, functools.partial(emit, c), deps) for c in range(CHUNKS_PER_SLAB)]

    def row_phase(cost, body, deps):
        return [add(vpu, cost, functools.partial(body, r), deps) for r in row_blocks]

    @pl.when(first_tile)
    def _():
        cbuf[0:HALO, :] = jnp.zeros((HALO, WIDTH), _F32)

    def rms_body(r):
        h_s[r:r + ROWS, :] = _rms_norm_rows(rows(x_ref, r), ng_ref[...]).astype(_BF16)
    t_rms = row_phase(30, rms_body, ())

    def w_in_chunk(j):
        return lambda c: win_ref[j * CHUNKS_PER_SLAB + c]

    def slab(dst, j, act):
        def finish(c, z):
            dst[:, chunk_cols(c)] = act(z).astype(dst.dtype)
        return matmul_chunks(h_s, w_in_chunk(j), t_rms, finish)

    def glu_emit(c):
        a = _dot(h_s[...], win_ref[SLAB_B_A * CHUNKS_PER_SLAB + c])
        b = _dot(h_s[...], win_ref[SLAB_B_B * CHUNKS_PER_SLAB + c])
        cbuf[HALO:HALO + tm, chunk_cols(c)] = a * _sigmoid(b)
    t_glu = [add(mxu, 1024 * scale, functools.partial(glu_emit, c), t_rms) for c in range(CHUNKS_PER_SLAB)]

    t_av = slab(z_av, SLAB_A_V, _gelu_tanh)

    def v_body(r):
        v_s[r:r + ROWS, :] = _layer_norm_rows(rows(z_av, r), lag_ref[...], lab_ref[...]).astype(_BF16)
    t_v = row_phase(50, v_body, t_av)

    def spatial_body(g):
        tri = (lax.broadcasted_iota(jnp.int32, (CHUNK, CHUNK), 0)
               >= lax.broadcasted_iota(jnp.int32, (CHUNK, CHUNK), 1))
        w_g = jnp.where(tri, ws_ref[g], jnp.zeros((), _BF16))
        cols = slice(g * GROUP_DIM, (g + 1) * GROUP_DIM)
        for c in range(0, tm, 2 * CHUNK):
            pair = jnp.concatenate([v_s[c:c + CHUNK, cols], v_s[c + CHUNK:c + 2 * CHUNK, cols]], axis=1)
            mixed = _dot(w_g, pair)
            sv_s[c:c + CHUNK, cols] = mixed[:, :GROUP_DIM]
            sv_s[c + CHUNK:c + 2 * CHUNK, cols] = mixed[:, GROUP_DIM:]
    t_sp = [add(mxu, 50 * (tm // CHUNK), functools.partial(spatial_body, g), t_v) for g in range(GROUPS)]

    t_au, t_ag = slab(z_au, SLAB_A_U, _gelu_tanh), slab(z_ag, SLAB_A_G, _silu)

    def bra_body(r):
        rc = r % CHUNK
        sv = rows(sv_s, r) + bs_ref[rc:rc + ROWS, :]
        bra_s[r:r + ROWS, :] = (rows(z_au, r) * sv * rows(z_ag, r)).astype(_BF16)
    t_bra = row_phase(30, bra_body, t_sp + t_au + t_ag)

    first = HALO - (CONV_WIDTH - 1)

    def conv_body(r, j, pin=None):
        y = None
        for b in range(SUBLANES):
            n = CONV_ROWS + (SUBLANES if b else 0)
            u = None
            for k in range(CONV_WIDTH):
                off = first + k
                if off % SUBLANES != b:
                    continue
                a = off - b
                w_k = cw_ref[k:k + 1, j:j + LANES]
                if pin is not None and u is None and b == 0:
                    w_k = jnp.where(pl.program_id(0) < 0, pin()[0:1, :], w_k)
                term = cbuf[r + a:r + a + n, j:j + LANES] * w_k
                u = term if u is None else u + term
            if b:
                u = pltpu.roll(u, n - b, axis=0)[0:CONV_ROWS, :]
            y = u if y is None else y + u
        y_s[r:r + CONV_ROWS, j:j + LANES] = y

    t_m0 = slab(z_m0, SLAB_MERGE + 0, _sigmoid)
    t_m1 = slab(z_m1, SLAB_MERGE + 1, _sigmoid)
    t_m2 = slab(z_m2, SLAB_MERGE + 2, _sigmoid)
    t_cg = slab(z_cg, SLAB_C_G, _silu)
    pinned = [(t, functools.partial(lambda buf, c: buf[0:SUBLANES, c * MXU_COLS:c * MXU_COLS + LANES], buf, c))
              for buf, tasks in ((z_m0, t_m0), (z_m1, t_m1), (z_m2, t_m2), (z_cg, t_cg))
              for c, t in enumerate(tasks)]
    blocks = [(r, j) for r in range(0, tm, CONV_ROWS) for j in range(0, WIDTH, LANES)]
    first_pinned, pin_every = 8, 3
    t_conv = []
    for i, (r, j) in enumerate(blocks):
        deps, pin = [t_glu[j // MXU_COLS]], None
        if i >= first_pinned and (i - first_pinned) % pin_every == 0 and (i - first_pinned) // pin_every < len(pinned):
            task, pin = pinned[(i - first_pinned) // pin_every]
            deps.append(task)
        t_conv.append(add(vpu, 170, functools.partial(conv_body, r, j, pin), deps))

    def carry_body():
        cbuf[0:HALO, :] = cbuf[tm:tm + HALO, :]
    add(vpu, 10, carry_body, t_conv)

    t_bg = slab(z_bg, SLAB_B_G, _silu)

    def brb_body(r):
        c = _silu(_layer_norm_rows(rows(y_s, r) + cb_ref[...], lbg_ref[...], lbb_ref[...]))
        brb_s[r:r + ROWS, :] = (c * rows(z_bg, r)).astype(_BF16)
    t_brb = row_phase(80, brb_body, t_conv + t_bg)

    def w_branch_chunk(n):
        return lambda c: wb_ref[n * CHUNKS_PER_SLAB + c]

    def merge_a(c, p):
        acc_s[:, chunk_cols(c)] = z_m0[:, chunk_cols(c)] * p
    t_pa = matmul_chunks(bra_s, w_branch_chunk(0), t_bra + t_m0, merge_a)


    def q_finish(c, z):
        q_s[:, chunk_cols(c)] = z.astype(_BF16)
    t_cq = matmul_chunks(h_s, w_in_chunk(SLAB_C_Q), t_rms, q_finish)
    nt_dims = (((1,), (1,)), ((), ()))

    def scores_body(hd):
        cols = slice(hd * HEAD_DIM, (hd + 1) * HEAD_DIM)
        sv_s[:, hd * MEM_LEN:(hd + 1) * MEM_LEN] = lax.dot_general(
            q_s[:, cols], k_ref[:, cols], nt_dims, preferred_element_type=_F32)
    t_sc = [add(mxu, 100 * scale, functools.partial(scores_body, hd), t_cq + t_bra) for hd in range(HEADS)]

    def softmax_body(r):
        for hd in range(HEADS):
            cols = slice(hd * MEM_LEN, (hd + 1) * MEM_LEN)
            s = sv_s[r:r + ROWS, cols]
            p = jnp.exp(s - jnp.max(s, axis=-1, keepdims=True))
            v_s[r:r + ROWS, cols] = (p * (1.0 / jnp.sum(p, axis=-1, keepdims=True))).astype(_BF16)
    t_sm = row_phase(40, softmax_body, t_sc + t_sp)

    def pv_body(hd):
        cols = slice(hd * HEAD_DIM, (hd + 1) * HEAD_DIM)
        att = _dot(v_s[:, hd * MEM_LEN:(hd + 1) * MEM_LEN], v_ref[:, cols])
        bra_s[:, cols] = (att * z_cg[:, cols]).astype(_BF16)
    t_pv = [add(mxu, 100 * scale, functools.partial(pv_body, hd), t_sm + t_cg + t_pa) for hd in range(HEADS)]

    def merge_c(c, p):
        acc_s[:, chunk_cols(c)] = acc_s[:, chunk_cols(c)] + z_m2[:, chunk_cols(c)] * p
    t_pc = matmul_chunks(bra_s, w_branch_chunk(2), t_pv + t_m2 + t_pa, merge_c)

    def merge_b(c, p):
        merged = acc_s[:, chunk_cols(c)] + z_m1[:, chunk_cols(c)] * p
        bra_s[:, chunk_cols(c)] = merged.astype(_BF16)
    t_pb = matmul_chunks(brb_s, w_branch_chunk(1), t_brb + t_m1 + t_pc, merge_b)

    if final:
        def out_finish(c, p):
            y_s[:, chunk_cols(c)] = x_ref[:, chunk_cols(c)] + p
        t_po = matmul_chunks(bra_s, lambda c: wo_ref[c], t_pb, out_finish)

        def out_body(r):
            o_ref[r:r + ROWS, :] = _rms_norm_rows(rows(y_s, r), fg_ref[...])
        row_phase(30, out_body, t_po)
    else:
        def out_finish(c, p):
            o_ref[:, chunk_cols(c)] = x_ref[:, chunk_cols(c)] + p
        t_po = matmul_chunks(bra_s, lambda c: wo_ref[c], t_pb, out_finish)

    mxu_order = (t_sp + t_pa + t_sc + t_pv + t_pc + t_pb + t_po
                 + t_glu + t_av + t_au + t_ag + t_cq + t_bg + t_m0 + t_cg + t_m1 + t_m2)
    assert len(mxu_order) == len(mxu)
    _trace_two_streams((mxu_order, vpu))


def _layer(x, k_all, v_all, layer, final, norm_g, w_in, lag, lab, w_s, bs_full, conv_w, conv_b,
           lbg, lbb, w_branch, w_out, final_g):
    batch, seq, _ = x.shape
    tm = TILE_TOKENS
    step = TILES_PER_STEP * tm
    assert seq % step == 0 and tm % CHUNK == 0 and tm % CONV_ROWS == 0

    def resident(shape):
        nd = len(shape)
        return pl.BlockSpec((None,) + shape, lambda b, s: (layer,) + (0,) * nd,
                            pipeline_mode=pl.Buffered(1))

    tile = pl.BlockSpec((None, step, D_MODEL), lambda b, s: (b, s, 0))
    kv_spec = pl.BlockSpec((None, None, MEM_LEN, WIDTH), lambda b, s: (layer, b, 0, 0))
    row = (1, WIDTH)
    f32_tile = pltpu.VMEM((tm, WIDTH), _F32)
    bf16_tile = pltpu.VMEM((tm, WIDTH), _BF16)
    return pl.pallas_call(
        functools.partial(_layer_kernel, tm=tm, final=final),
        grid=(batch, seq // step),
        in_specs=[
            tile, kv_spec, kv_spec,
            resident(row),
            resident((N_SLABS * CHUNKS_PER_SLAB, D_MODEL, MXU_COLS)),
            resident(row), resident(row),
            resident((GROUPS, CHUNK, CHUNK)),
            resident((CHUNK, WIDTH)),
            resident((CONV_WIDTH, WIDTH)),
            resident(row), resident(row), resident(row),
            resident((3 * CHUNKS_PER_SLAB, WIDTH, MXU_COLS)),
            resident((CHUNKS_PER_SLAB, D_MODEL, MXU_COLS)),
            pl.BlockSpec((1, D_MODEL), lambda b, s: (0, 0), pipeline_mode=pl.Buffered(1)),
        ],
        out_specs=tile,
        out_shape=jax.ShapeDtypeStruct(x.shape, x.dtype),
        scratch_shapes=[f32_tile] * 11 + [bf16_tile] * 5 + [pltpu.VMEM((tm + HALO, WIDTH), _F32)],
        compiler_params=pltpu.CompilerParams(
            dimension_semantics=("arbitrary", "arbitrary"),
            vmem_limit_bytes=VMEM_LIMIT_BYTES),
        name=f"trunk_layer_{layer}",
    )(x, k_all, v_all, norm_g, w_in, lag, lab, w_s, bs_full, conv_w, conv_b, lbg, lbb,
      w_branch, w_out, final_g)


def _chunk_kernel(w_ref, o_ref):
    for i in range(PREP_COLS // MXU_COLS):
        o_ref[i] = w_ref[:, i * MXU_COLS:(i + 1) * MXU_COLS].astype(_BF16)


def _column_chunks(w):
    n_mats, k, n = w.shape
    per_block = PREP_COLS // MXU_COLS
    return pl.pallas_call(
        _chunk_kernel,
        grid=(n_mats, n // PREP_COLS),
        in_specs=[pl.BlockSpec((None, k, PREP_COLS), lambda l, j: (l, 0, j))],
        out_specs=pl.BlockSpec((None, per_block, k, MXU_COLS), lambda l, j: (l, j, 0, 0)),
        out_shape=jax.ShapeDtypeStruct((n_mats, n // MXU_COLS, k, MXU_COLS), _BF16),
        compiler_params=pltpu.CompilerParams(dimension_semantics=("arbitrary", "arbitrary")),
        name="weight_chunks",
    )(w)


def kernel(x, mem, norm_g, mem_norm_g, w_in, gmlp_ln_g, gmlp_ln_b, w_s, b_s, conv_w, conv_b,
           conv_ln_g, conv_ln_b, w_kv, w_branch, w_out, final_norm_g):
    depth = w_in.shape[0]
    vec = lambda a: a.reshape(depth, 1, a.shape[-1])
    bs_full = jnp.repeat(jnp.swapaxes(b_s, 1, 2), GROUP_DIM, axis=2)
    k_all, v_all = _keys_values(mem, mem_norm_g, w_kv)
    w_s_b = w_s.astype(_BF16)
    w_in_b = _column_chunks(w_in)
    w_branch_b = _column_chunks(w_branch.reshape(-1, WIDTH, D_MODEL)).reshape(depth, -1, WIDTH, MXU_COLS)
    w_out_b = _column_chunks(w_out)
    final_g = final_norm_g.reshape(1, D_MODEL)
    for layer in range(depth):
        x = _layer(x, k_all, v_all, layer, layer == depth - 1, vec(norm_g), w_in_b,
                   vec(gmlp_ln_g), vec(gmlp_ln_b), w_s_b, bs_full, conv_w, vec(conv_b),
                   vec(conv_ln_g), vec(conv_ln_b), w_branch_b, w_out_b, final_g)
    return x
```
